```python
import jax, jax.numpy as jnp
from jax import lax
import numpy as np

D_MODEL = 1024
BATCH = 2
SEQ = 16384
DEPTH = 4

N_MIXERS = 2
N_ATTN_LAYERS = (DEPTH + 1) // 2
N_GDN_LAYERS = DEPTH // 2
D_FF = 2816
NORM_EPS = 1e-6

ATTN_Q_HEADS = 16
ATTN_KV_HEADS = 4
ATTN_HEAD_DIM = 64
ATTN_GROUP = ATTN_Q_HEADS // ATTN_KV_HEADS
WINDOW = 128
ATTN_BLOCK = 128
ROPE_DIM = ATTN_HEAD_DIM // 4
ROPE_THETA = 500000.0
ATTN_Q_W = ATTN_Q_HEADS * ATTN_HEAD_DIM
ATTN_KV_W = ATTN_KV_HEADS * ATTN_HEAD_DIM
ATTN_IN = ATTN_Q_W + 2 * ATTN_KV_W

GDN_HEADS = 8
GDN_DK = 128
GDN_DV = 128
GDN_CONV = 4
GDN_CHUNK = 64
GDN_QK_W = GDN_HEADS * GDN_DK
GDN_V_W = GDN_HEADS * GDN_DV
GDN_CONV_W = 2 * GDN_QK_W + GDN_V_W
GDN_IN = GDN_CONV_W + GDN_V_W + 2 * GDN_HEADS

kernel_name = "hybrid_swa_sink_gdn_macaron"


def rms_norm(x, w):
    xf = x.astype(jnp.float32)
    y = xf * lax.rsqrt(jnp.mean(xf * xf, axis=-1, keepdims=True) + NORM_EPS)
    return (y * w.astype(jnp.float32)).astype(x.dtype)


def swiglu_ffn(h, w_gate_up, w_down):
    gate, up = jnp.split(h @ w_gate_up, 2, axis=-1)
    return (jax.nn.silu(gate) * up) @ w_down


def partial_rope(t, cos, sin):
    tf = t.astype(jnp.float32)
    half = ROPE_DIM // 2
    r1, r2, rest = tf[..., :half], tf[..., half:ROPE_DIM], tf[..., ROPE_DIM:]
    out = jnp.concatenate([r1 * cos - r2 * sin, r2 * cos + r1 * sin, rest], axis=-1)
    return out.astype(t.dtype)


def swa_sink_attention(h, cos, sin, w_in, b_in, sinks, w_out, b_out):
    B, S, _ = h.shape
    nb = S // ATTN_BLOCK
    qkv = h @ w_in + b_in
    q, k, v = jnp.split(qkv, [ATTN_Q_W, ATTN_Q_W + ATTN_KV_W], axis=-1)
    q = partial_rope(q.reshape(B, S, ATTN_Q_HEADS, ATTN_HEAD_DIM), cos, sin)
    k = partial_rope(k.reshape(B, S, ATTN_KV_HEADS, ATTN_HEAD_DIM), cos, sin)
    v = v.reshape(B, S, ATTN_KV_HEADS, ATTN_HEAD_DIM)
    qb = q.reshape(B, nb, ATTN_BLOCK, ATTN_KV_HEADS, ATTN_GROUP, ATTN_HEAD_DIM)

    def band(t):
        cur = t.reshape(B, nb, ATTN_BLOCK, ATTN_KV_HEADS, ATTN_HEAD_DIM)
        prev = jnp.pad(cur, ((0, 0), (1, 0), (0, 0), (0, 0), (0, 0)))[:, :-1]
        return jnp.concatenate([prev, cur], axis=2)

    kb, vb = band(k), band(v)
    scores = jnp.einsum('bnqhgd,bnkhd->bnhgqk', qb, kb,
                        preferred_element_type=jnp.float32) * (ATTN_HEAD_DIM ** -0.5)
    qi = jnp.arange(ATTN_BLOCK)[:, None]
    kj = jnp.arange(2 * ATTN_BLOCK)[None, :]
    rel = qi + ATTN_BLOCK - kj
    in_window = (rel >= 0) & (rel < WINDOW)
    blk = jnp.arange(nb)[:, None, None]
    valid = in_window[None] & ((blk - 1) * ATTN_BLOCK + kj[None] >= 0)
    scores = jnp.where(valid[None, :, None, None], scores, -jnp.inf)
    sink = sinks.astype(jnp.float32).reshape(ATTN_KV_HEADS, ATTN_GROUP)[None, None, :, :, None, None]
    m = jnp.maximum(jnp.max(scores, axis=-1, keepdims=True), sink)
    p = jnp.exp(scores - m)
    probs = (p / (jnp.sum(p, axis=-1, keepdims=True) + jnp.exp(sink - m))).astype(v.dtype)
    o = jnp.einsum('bnhgqk,bnkhd->bnqhgd', probs, vb).reshape(B, S, ATTN_Q_W)
    return o @ w_out + b_out


def l2_normalize(t):
    return t * lax.rsqrt(jnp.sum(t * t, axis=-1, keepdims=True) + NORM_EPS)


def chunk_gated_delta_rule(q, k, v, g, beta):
    B, S, H, DK = q.shape
    DV = v.shape[-1]
    C = GDN_CHUNK
    N = S // C

    def to_chunks(t):
        return jnp.moveaxis(t.reshape(B, N, C, H, *t.shape[3:]), 3, 1)

    q, k, v, g, beta = (to_chunks(t) for t in (q, k, v, g, beta))
    decay = jnp.cumsum(g, axis=-1)
    causal = jnp.tril(jnp.ones((C, C), dtype=bool))
    strict = jnp.tril(jnp.ones((C, C), dtype=bool), -1)
    L = jnp.exp(jnp.where(causal, decay[..., :, None] - decay[..., None, :], -jnp.inf))
    k_beta = k * beta[..., None]
    A = jnp.where(strict, jnp.einsum('bhncd,bhnsd->bhncs', k_beta, k) * L, 0.0)
    eye = jnp.eye(C, dtype=jnp.float32)
    rhs = jnp.concatenate([v * beta[..., None], k_beta * jnp.exp(decay)[..., None]], axis=-1)
    sol = lax.linalg.triangular_solve(eye + A, rhs, left_side=True, lower=True)
    U, W = sol[..., :DV], sol[..., DV:]
    Aqk = jnp.where(causal, jnp.einsum('bhncd,bhnsd->bhncs', q, k) * L, 0.0)
    decay_last = decay[..., -1:]
    q_dec = q * jnp.exp(decay)[..., None]
    k_dec = k * jnp.exp(decay_last - decay)[..., None]
    chunk_decay = jnp.exp(decay_last[..., 0])
    xs = tuple(jnp.moveaxis(t, 2, 0) for t in (q_dec, k_dec, U, W, Aqk, chunk_decay))

    def step(state, inp):
        qd, kd, u, w, aqk, cd = inp
        v_new = u - jnp.einsum('bhcd,bhde->bhce', w, state)
        o = jnp.einsum('bhcd,bhde->bhce', qd, state) + jnp.einsum('bhcs,bhse->bhce', aqk, v_new)
        state = state * cd[..., None, None] + jnp.einsum('bhcd,bhce->bhde', kd, v_new)
        return state, o

    _, o = lax.scan(step, jnp.zeros((B, H, DK, DV), jnp.float32), xs)
    return jnp.transpose(o, (1, 0, 3, 2, 4)).reshape(B, S, H, DV)


def gated_deltanet(h, w_in, conv_w, A_log, dt_bias, norm_w, w_out):
    B, S, _ = h.shape
    proj = h @ w_in
    qkv, z, b, a = jnp.split(proj, [GDN_CONV_W, GDN_CONV_W + GDN_V_W, GDN_CONV_W + GDN_V_W + GDN_HEADS], axis=-1)
    qkv = lax.conv_general_dilated(qkv, conv_w[:, None, :].astype(qkv.dtype), window_strides=(1,),
                                   padding=[(GDN_CONV - 1, 0)], dimension_numbers=('NWC', 'WIO', 'NWC'),
                                   feature_group_count=GDN_CONV_W)
    qkv = jax.nn.silu(qkv).astype(jnp.float32)
    q, k, v = jnp.split(qkv, [GDN_QK_W, 2 * GDN_QK_W], axis=-1)
    q = l2_normalize(q.reshape(B, S, GDN_HEADS, GDN_DK)) * (GDN_DK ** -0.5)
    k = l2_normalize(k.reshape(B, S, GDN_HEADS, GDN_DK))
    v = v.reshape(B, S, GDN_HEADS, GDN_DV)
    beta = jax.nn.sigmoid(b.astype(jnp.float32))
    g = -jnp.exp(A_log.astype(jnp.float32)) * jax.nn.softplus(a.astype(jnp.float32) + dt_bias.astype(jnp.float32))
    o = chunk_gated_delta_rule(q, k, v, g, beta)
    zf = z.reshape(B, S, GDN_HEADS, GDN_DV).astype(jnp.float32)
    o = o * lax.rsqrt(jnp.mean(o * o, axis=-1, keepdims=True) + NORM_EPS) * norm_w.astype(jnp.float32) * jax.nn.silu(zf)
    return o.astype(h.dtype).reshape(B, S, GDN_V_W) @ w_out


def setup_inputs(seed: int = 0) -> dict:
    key = jax.random.key(seed)
    ks = jax.random.split(key, 24)
    f32 = jnp.float32

    def nrm(k, shape, fan_in):
        return jax.random.normal(k, shape, f32) * (fan_in ** -0.5)

    def gain(k, shape):
        return 1.0 + 0.05 * jax.random.normal(k, shape, f32)

    x = jax.random.normal(ks[0], (BATCH, SEQ, D_MODEL), f32)
    positions = (jnp.arange(SEQ, dtype=jnp.int32)[None, :]
                 + jax.random.randint(ks[1], (BATCH, 1), 0, 4096, dtype=jnp.int32))
    dt = jnp.exp(jax.random.uniform(ks[17], (N_GDN_LAYERS, GDN_HEADS), f32, np.log(1e-3), np.log(1e-1)))
    return {
        "x": x,
        "positions": positions,
        "ffn1_norm": gain(ks[2], (DEPTH, D_MODEL)),
        "ffn1_w_gate_up": nrm(ks[3], (DEPTH, D_MODEL, 2 * D_FF), D_MODEL),
        "ffn1_w_down": nrm(ks[4], (DEPTH, D_FF, D_MODEL), D_FF),
        "mix_norm": gain(ks[5], (DEPTH, D_MODEL)),
        "ffn2_norm": gain(ks[6], (DEPTH, D_MODEL)),
        "ffn2_w_gate_up": nrm(ks[7], (DEPTH, D_MODEL, 2 * D_FF), D_MODEL),
        "ffn2_w_down": nrm(ks[8], (DEPTH, D_FF, D_MODEL), D_FF),
        "attn_w_in": nrm(ks[9], (N_ATTN_LAYERS, D_MODEL, ATTN_IN), D_MODEL),
        "attn_b_in": 0.02 * jax.random.normal(ks[10], (N_ATTN_LAYERS, ATTN_IN), f32),
        "attn_sinks": jax.random.normal(ks[11], (N_ATTN_LAYERS, ATTN_Q_HEADS), f32),
        "attn_w_out": nrm(ks[12], (N_ATTN_LAYERS, ATTN_Q_W, D_MODEL), ATTN_Q_W),
        "attn_b_out": 0.02 * jax.random.normal(ks[13], (N_ATTN_LAYERS, D_MODEL), f32),
        "gdn_w_in": nrm(ks[14], (N_GDN_LAYERS, D_MODEL, GDN_IN), D_MODEL),
        "gdn_conv_w": nrm(ks[15], (N_GDN_LAYERS, GDN_CONV, GDN_CONV_W), GDN_CONV),
        "gdn_A_log": jnp.log(jax.random.uniform(ks[16], (N_GDN_LAYERS, GDN_HEADS), f32, 1.0, 16.0)),
        "gdn_dt_bias": dt + jnp.log(-jnp.expm1(-dt)),
        "gdn_norm_w": gain(ks[18], (N_GDN_LAYERS, GDN_DV)),
        "gdn_w_out": nrm(ks[19], (N_GDN_LAYERS, GDN_V_W, D_MODEL), GDN_V_W),
        "final_norm": gain(ks[20], (D_MODEL,)),
    }


def reference(x, positions, ffn1_norm, ffn1_w_gate_up, ffn1_w_down, mix_norm, ffn2_norm,
              ffn2_w_gate_up, ffn2_w_down, attn_w_in, attn_b_in, attn_sinks, attn_w_out,
              attn_b_out, gdn_w_in, gdn_conv_w, gdn_A_log, gdn_dt_bias, gdn_norm_w, gdn_w_out,
              final_norm):
    inv_freq = ROPE_THETA ** (-jnp.arange(0, ROPE_DIM, 2, dtype=jnp.float32) / ROPE_DIM)
    ang = positions.astype(jnp.float32)[..., None] * inv_freq
    cos, sin = jnp.cos(ang)[:, :, None, :], jnp.sin(ang)[:, :, None, :]
    h = x
    for layer in range(DEPTH):
        h = h + 0.5 * swiglu_ffn(rms_norm(h, ffn1_norm[layer]), ffn1_w_gate_up[layer], ffn1_w_down[layer])
        hn = rms_norm(h, mix_norm[layer])
        j = layer // N_MIXERS
        if layer % N_MIXERS == 0:
            h = h + swa_sink_attention(hn, cos, sin, attn_w_in[j], attn_b_in[j], attn_sinks[j],
                                       attn_w_out[j], attn_b_out[j])
        else:
            h = h + gated_deltanet(hn, gdn_w_in[j], gdn_conv_w[j], gdn_A_log[j], gdn_dt_bias[j],
                                   gdn_norm_w[j], gdn_w_out[j])
        h = h + 0.5 * swiglu_ffn(rms_norm(h, ffn2_norm[layer]), ffn2_w_gate_up[layer], ffn2_w_down[layer])
    return rms_norm(h, final_norm)
```

```python
import functools

import jax
import jax.numpy as jnp
from jax import lax
from jax.experimental import pallas as pl
from jax.experimental.pallas import tpu as pltpu

F32 = jnp.float32
BF16 = jnp.bfloat16

NORM_EPS = 1e-6

ATTN_Q_HEADS = 16
ATTN_KV_HEADS = 4
ATTN_HEAD_DIM = 64
ATTN_GROUP = ATTN_Q_HEADS // ATTN_KV_HEADS
ATTN_BLOCK = 128
ROPE_DIM = ATTN_HEAD_DIM // 4
ROPE_THETA = 500000.0
GDN_HEADS = 8
GDN_DK = 128
GDN_DV = 128
GDN_CONV = 4
GDN_CHUNK = 128

V7X_LANES = 128
V7X_SUBLANES = 8
V7X_VMEM_BYTES = 64 * 1024 * 1024

FFN_TM = 512
FFN_FF_CHUNK = 256
QKV_TM = 512
ATTN_TQ = 256
GDN_IN_TM = 256


def _vmem_limit(estimate_bytes):
    return int(min(V7X_VMEM_BYTES - (4 << 20), max(estimate_bytes, 16 << 20)))


def _dot(a, b):
    return jnp.dot(a, b, preferred_element_type=F32)


def _dot_nt(a, b):
    return lax.dot_general(a, b, (((1,), (1,)), ((), ())), preferred_element_type=F32)


def _dot_tn(a, b):
    return lax.dot_general(a, b, (((0,), (0,)), ((), ())), preferred_element_type=F32)


def _rms(x, w):
    ms = jnp.mean(x * x, axis=-1, keepdims=True)
    return x * lax.rsqrt(ms + NORM_EPS) * w


def _silu(x):
    return x * jax.nn.sigmoid(x)


def _const_spec(shape, index_map):
    return pl.BlockSpec(shape, index_map, pipeline_mode=pl.Buffered(1))


def _ffn_kernel(*refs, d_ff, ff_chunk, has_proj, has_bias, has_final):
    refs = list(refs)
    h_ref = refs.pop(0)
    if has_proj:
        o_ref = refs.pop(0)
        wo_ref = refs.pop(0)
    if has_bias:
        bo_ref = refs.pop(0)
    nw_ref = refs.pop(0)
    wgu_ref = refs.pop(0)
    wd_ref = refs.pop(0)
    if has_final:
        fw_ref = refs.pop(0)
    out_ref, act_ref = refs

    x = h_ref[...]
    if has_proj:
        x = x + _dot(o_ref[...], wo_ref[...])
    if has_bias:
        x = x + bo_ref[...]
    xn = _rms(x, nw_ref[...]).astype(BF16)
    for c in range(d_ff // ff_chunk):
        lo = c * ff_chunk
        g = _dot(xn, wgu_ref[:, lo:lo + ff_chunk])
        u = _dot(xn, wgu_ref[:, d_ff + lo:d_ff + lo + ff_chunk])
        act_ref[:, lo:lo + ff_chunk] = (_silu(g) * u).astype(BF16)
    y = x + 0.5 * _dot(act_ref[...], wd_ref[...])
    if has_final:
        y = _rms(y, fw_ref[...])
    out_ref[...] = y


def _ffn(h, norm_w, wgu, wd, layer, proj=None, final_w=None):
    T, D = h.shape
    d_ff = wd.shape[1]
    tm = min(FFN_TM, T)
    ff_chunk = FFN_FF_CHUNK if d_ff % FFN_FF_CHUNK == 0 else d_ff
    args = [h]
    specs = [pl.BlockSpec((tm, D), lambda i: (i, 0))]
    est = 4 * tm * D * 4 + wgu.shape[1] * wgu.shape[2] * 2 + d_ff * D * 2 + tm * d_ff * 2 + 6 * tm * ff_chunk * 4
    has_proj = proj is not None
    has_bias = has_proj and proj[3] is not None
    if has_proj:
        o, w_out, j, bias = proj
        P = o.shape[1]
        args += [o, w_out]
        specs += [pl.BlockSpec((tm, P), lambda i: (i, 0)),
                  _const_spec((None, P, D), lambda i: (j, 0, 0))]
        est += 2 * tm * P * 2 + P * D * 2
        if has_bias:
            args.append(bias)
            specs.append(_const_spec((None, 1, D), lambda i: (j, 0, 0)))
    args += [norm_w, wgu, wd]
    specs += [_const_spec((None, 1, D), lambda i: (layer, 0, 0)),
              _const_spec((None, D, 2 * d_ff), lambda i: (layer, 0, 0)),
              _const_spec((None, d_ff, D), lambda i: (layer, 0, 0))]
    if final_w is not None:
        args.append(final_w)
        specs.append(_const_spec((1, D), lambda i: (0, 0)))
    kern = functools.partial(_ffn_kernel, d_ff=d_ff, ff_chunk=ff_chunk, has_proj=has_proj,
                             has_bias=has_bias, has_final=final_w is not None)
    return pl.pallas_call(
        kern,
        grid=(T // tm,),
        in_specs=specs,
        out_specs=pl.BlockSpec((tm, D), lambda i: (i, 0)),
        out_shape=jax.ShapeDtypeStruct((T, D), F32),
        scratch_shapes=[pltpu.VMEM((tm, d_ff), BF16)],
        compiler_params=pltpu.CompilerParams(dimension_semantics=("parallel",),
                                             vmem_limit_bytes=_vmem_limit(est + (8 << 20))),
        name="ffn",
    )(*args)


def _attn_qkv_kernel(h_ref, pos_ref, invf_ref, nw_ref, w_ref, b_ref, q_ref, k_ref, v_ref, *, q_w, kv_w):
    xn = _rms(h_ref[...], nw_ref[...]).astype(BF16)
    qkv = _dot(xn, w_ref[...]) + b_ref[...]
    ang = pos_ref[...].astype(F32) * invf_ref[...]
    cos = jnp.cos(ang)
    sin = jnp.sin(ang)
    lane = lax.broadcasted_iota(jnp.int32, (1, V7X_LANES), 1)
    d = lane % ATTN_HEAD_DIM
    first_half = d < (ROPE_DIM // 2)
    sin_signed = jnp.where(first_half, -sin, sin)
    half = ROPE_DIM // 2

    def rope(t):
        up = pltpu.roll(t, V7X_LANES - half, axis=1)
        dn = pltpu.roll(t, half, axis=1)
        return t * cos + jnp.where(first_half, up, dn) * sin_signed

    scale = ATTN_HEAD_DIM ** -0.5
    for s in range(q_w // V7X_LANES):
        sl = slice(s * V7X_LANES, (s + 1) * V7X_LANES)
        q_ref[:, sl] = (rope(qkv[:, sl]) * scale).astype(BF16)
    low = lane < ATTN_HEAD_DIM
    for s in range(kv_w // V7X_LANES):
        ks = rope(qkv[:, q_w + s * V7X_LANES:q_w + (s + 1) * V7X_LANES])
        vs = qkv[:, q_w + kv_w + s * V7X_LANES:q_w + kv_w + (s + 1) * V7X_LANES]
        for t, ref in ((ks, k_ref), (vs, v_ref)):
            sw = pltpu.roll(t, ATTN_HEAD_DIM, axis=1)
            ref[:, (2 * s) * V7X_LANES:(2 * s + 1) * V7X_LANES] = jnp.where(low, t, sw).astype(BF16)
            ref[:, (2 * s + 1) * V7X_LANES:(2 * s + 2) * V7X_LANES] = jnp.where(low, sw, t).astype(BF16)


def _attn_qkv(h, pos, invf, norm_w, w_in, b_in, layer, j):
    T, D = h.shape
    q_w = ATTN_Q_HEADS * ATTN_HEAD_DIM
    kv_w = ATTN_KV_HEADS * ATTN_HEAD_DIM
    n_in = q_w + 2 * kv_w
    rep_w = ATTN_KV_HEADS * V7X_LANES
    tm = min(QKV_TM, T)
    est = 2 * tm * D * 4 + D * n_in * 2 + 3 * tm * n_in * 4 + 2 * tm * (q_w + 2 * rep_w) * 2
    kern = functools.partial(_attn_qkv_kernel, q_w=q_w, kv_w=kv_w)
    return pl.pallas_call(
        kern,
        grid=(T // tm,),
        in_specs=[pl.BlockSpec((tm, D), lambda i: (i, 0)),
                  pl.BlockSpec((tm, 1), lambda i: (i, 0)),
                  _const_spec((1, V7X_LANES), lambda i: (0, 0)),
                  _const_spec((None, 1, D), lambda i: (layer, 0, 0)),
                  _const_spec((None, D, n_in), lambda i: (j, 0, 0)),
                  _const_spec((None, 1, n_in), lambda i: (j, 0, 0))],
        out_specs=[pl.BlockSpec((tm, q_w), lambda i: (i, 0)),
                   pl.BlockSpec((tm, rep_w), lambda i: (i, 0)),
                   pl.BlockSpec((tm, rep_w), lambda i: (i, 0))],
        out_shape=[jax.ShapeDtypeStruct((T, q_w), BF16),
                   jax.ShapeDtypeStruct((T, rep_w), BF16),
                   jax.ShapeDtypeStruct((T, rep_w), BF16)],
        compiler_params=pltpu.CompilerParams(dimension_semantics=("parallel",),
                                             vmem_limit_bytes=_vmem_limit(est + (8 << 20))),
        name="attn_qkv",
    )(h, pos, invf, norm_w, w_in, b_in)


def _attn_core_kernel(sink_ref, q_ref, kc_ref, kp_ref, vc_ref, vp_ref, o_ref, *, tq):
    blk = ATTN_BLOCK
    first_tile = pl.program_id(1) == 0
    row = lax.broadcasted_iota(jnp.int32, (blk, blk), 0)
    col = lax.broadcasted_iota(jnp.int32, (blk, blk), 1)
    tril = col <= row
    lane = lax.broadcasted_iota(jnp.int32, (1, V7X_LANES), 1)
    half_mask = (lane < ATTN_HEAD_DIM, lane >= ATTN_HEAD_DIM)
    no_prev = jnp.where(first_tile, -jnp.inf, 0.0).astype(F32)
    for j in range(tq // blk):
        rows = slice(j * blk, (j + 1) * blk)
        prows = slice((j - 1) * blk, j * blk)
        for h in range(ATTN_KV_HEADS):
            hs = slice(h * V7X_LANES, (h + 1) * V7X_LANES)
            kp = kp_ref[:, hs] if j == 0 else kc_ref[prows, hs]
            vp = vp_ref[:, hs] if j == 0 else vc_ref[prows, hs]
            kband = jnp.concatenate([kp, kc_ref[rows, hs]], axis=0)
            vband = jnp.concatenate([vp, vc_ref[rows, hs]], axis=0)
            km = [jnp.where(m, kband, jnp.zeros_like(kband)) for m in half_mask]
            vm = [jnp.where(m, vband, jnp.zeros_like(vband)) for m in half_mask]
            for p in range(ATTN_GROUP // 2):
                qsl = slice(h * ATTN_GROUP * ATTN_HEAD_DIM + p * V7X_LANES,
                            h * ATTN_GROUP * ATTN_HEAD_DIM + (p + 1) * V7X_LANES)
                qs = q_ref[rows, qsl]
                acc = jnp.zeros((blk, V7X_LANES), F32)
                for e in range(2):
                    head = h * ATTN_GROUP + 2 * p + e
                    s = _dot_nt(qs, km[e])
                    s_prev = s[:, :blk]
                    if j == 0:
                        s_prev = s_prev + no_prev
                    sc = jnp.where(tril, s[:, blk:], s_prev)
                    sink = sink_ref[head]
                    m = jnp.maximum(jnp.max(sc, axis=-1, keepdims=True), sink)
                    pr = jnp.exp(sc - m)
                    denom = jnp.sum(pr, axis=-1, keepdims=True) + jnp.exp(sink - m)
                    pn = pr / denom
                    zero = jnp.zeros_like(pn)
                    pband = jnp.concatenate([jnp.where(tril, zero, pn), jnp.where(tril, pn, zero)],
                                            axis=1).astype(BF16)
                    acc = acc + _dot(pband, vm[e])
                o_ref[rows, qsl] = acc.astype(BF16)


def _attn_core(q, krep, vrep, sinks):
    B, S, q_w = q.shape
    rep_w = krep.shape[-1]
    tq = min(ATTN_TQ, S)
    nb = tq // ATTN_BLOCK
    kern = functools.partial(_attn_core_kernel, tq=tq)
    cur = lambda b, i: (b, i, 0)
    prev = lambda b, i: (b, jnp.maximum(i * nb - 1, 0), 0)
    return pl.pallas_call(
        kern,
        grid=(B, S // tq),
        in_specs=[pl.BlockSpec(memory_space=pltpu.SMEM),
                  pl.BlockSpec((None, tq, q_w), cur),
                  pl.BlockSpec((None, tq, rep_w), cur),
                  pl.BlockSpec((None, ATTN_BLOCK, rep_w), prev),
                  pl.BlockSpec((None, tq, rep_w), cur),
                  pl.BlockSpec((None, ATTN_BLOCK, rep_w), prev)],
        out_specs=pl.BlockSpec((None, tq, q_w), cur),
        out_shape=jax.ShapeDtypeStruct((B, S, q_w), BF16),
        compiler_params=pltpu.CompilerParams(dimension_semantics=("parallel", "parallel")),
        name="attn_core",
    )(sinks, q, krep, krep, vrep, vrep)


def _split3(x):
    a = x.astype(BF16)
    r = x - a.astype(F32)
    b = r.astype(BF16)
    c = (r - b.astype(F32)).astype(BF16)
    return a, b, c


def _gdn_in_kernel(h_ref, nw_ref, w_ref, wb_ref, wa_ref, cw_ref, alog_ref, dtb_ref,
                   q_ref, k_ref, kb_ref, vb_ref, kbd_ref, qd_ref, kd_ref, zg_ref, l_ref, cd_ref,
                   buf_ref, *, tm):
    C = GDN_CHUNK
    qk_w = GDN_HEADS * GDN_DK
    conv_w = 2 * qk_w + GDN_HEADS * GDN_DV
    halo = V7X_SUBLANES

    @pl.when(pl.program_id(1) == 0)
    def _():
        buf_ref[0:halo, :] = jnp.zeros((halo, conv_w), F32)

    xn = _rms(h_ref[...], nw_ref[...]).astype(BF16)
    buf_ref[halo:halo + tm, :] = _dot(xn, w_ref[:, :conv_w])
    zg_ref[...] = _silu(_dot(xn, w_ref[:, conv_w:])).astype(BF16)
    beta = jax.nn.sigmoid(_dot(xn, wb_ref[...]))
    a = _dot(xn, wa_ref[...])
    g = -jnp.exp(alog_ref[...]) * jax.nn.softplus(a + dtb_ref[...])

    row = lax.broadcasted_iota(jnp.int32, (C, C), 0)
    col = lax.broadcasted_iota(jnp.int32, (C, C), 1)
    causal = col <= row
    tril = causal.astype(BF16)

    def conv_silu(lo):
        acc = None
        for i in range(GDN_CONV):
            off = halo - (GDN_CONV - 1) + i
            term = cw_ref[i:i + 1, lo:lo + V7X_LANES] * buf_ref[off:off + tm, lo:lo + V7X_LANES]
            acc = term if acc is None else acc + term
        return _silu(acc)

    def l2n(t):
        return t * lax.rsqrt(jnp.sum(t * t, axis=-1, keepdims=True) + NORM_EPS)

    d_cols, d_rows = [], []
    for c in range(tm // C):
        g1, g2, g3 = _split3(g[c * C:(c + 1) * C, :])
        dc = _dot(tril, g1) + _dot(tril, g2) + _dot(tril, g3)
        d_cols.append(dc)
        d_rows.append(dc.T)

    for h in range(GDN_HEADS):
        hs = slice(h * GDN_DK, (h + 1) * GDN_DK)
        qh = l2n(conv_silu(h * GDN_DK)) * (GDN_DK ** -0.5)
        kh = l2n(conv_silu(qk_w + h * GDN_DK))
        vh = conv_silu(2 * qk_w + h * GDN_DV)
        for c in range(tm // C):
            rs = slice(c * C, (c + 1) * C)
            dB = jnp.broadcast_to(d_cols[c][:, h:h + 1], (C, C))
            bB = jnp.broadcast_to(beta[rs, h:h + 1], (C, C))
            dR = jnp.broadcast_to(d_rows[c][h:h + 1, :], (C, C))
            dL = jnp.broadcast_to(dB[C - 1:C, :], (C, C))
            eB = jnp.exp(dB)
            q_c, k_c, v_c = qh[rs], kh[rs], vh[rs]
            kb = k_c * bB
            q_ref[rs, hs] = q_c.astype(BF16)
            k_ref[rs, hs] = k_c.astype(BF16)
            kb_ref[rs, hs] = kb.astype(BF16)
            vb_ref[rs, hs] = (v_c * bB).astype(BF16)
            kbd_ref[rs, hs] = (kb * eB).astype(BF16)
            qd_ref[rs, hs] = (q_c * eB).astype(BF16)
            kd_ref[rs, hs] = (k_c * jnp.exp(dL - dB)).astype(BF16)
            l_ref[rs, hs] = jnp.exp(jnp.where(causal, dB - dR, -jnp.inf))
            cd_ref[c, :, hs] = jnp.exp(dL[0:1, :])

    buf_ref[0:halo, :] = buf_ref[tm:tm + halo, :]


def _gdn_in(h, norm_w, w_qkvz, w_b, w_a, conv_w, alog, dtb, layer, j):
    B, S, D = h.shape
    hw = GDN_HEADS * GDN_DK
    conv_cols = conv_w.shape[-1]
    tm = min(GDN_IN_TM, S)
    n_chunks = S // GDN_CHUNK
    tile = lambda b, i: (b, i, 0)
    big = pl.BlockSpec((None, tm, hw), tile)
    est = (2 * tm * D * 4 + D * w_qkvz.shape[-1] * 2 + 2 * D * V7X_LANES * 2 + (tm + 8) * conv_cols * 4
           + 2 * 8 * tm * hw * 2 + 2 * tm * hw * 4 + 3 * tm * hw * 4)
    kern = functools.partial(_gdn_in_kernel, tm=tm)
    outs = pl.pallas_call(
        kern,
        grid=(B, S // tm),
        in_specs=[pl.BlockSpec((None, tm, D), tile),
                  _const_spec((None, 1, D), lambda b, i: (layer, 0, 0)),
                  _const_spec((None, D, w_qkvz.shape[-1]), lambda b, i: (j, 0, 0)),
                  _const_spec((None, D, V7X_LANES), lambda b, i: (j, 0, 0)),
                  _const_spec((None, D, V7X_LANES), lambda b, i: (j, 0, 0)),
                  _const_spec((None, GDN_CONV, conv_cols), lambda b, i: (j, 0, 0)),
                  _const_spec((None, 1, V7X_LANES), lambda b, i: (j, 0, 0)),
                  _const_spec((None, 1, V7X_LANES), lambda b, i: (j, 0, 0))],
        out_specs=[big] * 8 + [pl.BlockSpec((None, tm, hw), tile),
                               pl.BlockSpec((None, tm // GDN_CHUNK, 1, hw), lambda b, i: (b, i, 0, 0))],
        out_shape=[jax.ShapeDtypeStruct((B, S, hw), BF16)] * 8
                  + [jax.ShapeDtypeStruct((B, S, hw), F32),
                     jax.ShapeDtypeStruct((B, n_chunks, 1, hw), F32)],
        scratch_shapes=[pltpu.VMEM((tm + V7X_SUBLANES, conv_cols), F32)],
        compiler_params=pltpu.CompilerParams(dimension_semantics=("arbitrary", "arbitrary"),
                                             vmem_limit_bytes=_vmem_limit(est + (8 << 20))),
        name="gdn_in",
    )(h, norm_w, w_qkvz, w_b, w_a, conv_w, alog, dtb)
    return outs


def _unit_lower_inverse(a_strict, eye):
    C = a_strict.shape[0]
    p = eye - a_strict
    x = a_strict
    levels = C.bit_length() - 2
    for _ in range(levels):
        xb = x.astype(BF16)
        x = _dot(xb, xb)
        p = p + _dot(p.astype(BF16), x.astype(BF16))
    a_hi = a_strict.astype(BF16)
    a_lo = (a_strict - a_hi.astype(F32)).astype(BF16)
    p_hi = p.astype(BF16)
    p_lo = (p - p_hi.astype(F32)).astype(BF16)
    resid = eye - p - (_dot(a_hi, p_hi) + _dot(a_hi, p_lo) + _dot(a_lo, p_hi))
    return p + _dot(p_hi, resid.astype(BF16))


def _gdn_core_kernel(q_ref, k_ref, kb_ref, vb_ref, kbd_ref, qd_ref, kd_ref, zg_ref, l_ref, cd_ref, nw_ref,
                     o_ref, s_ref):
    C = GDN_CHUNK

    @pl.when(pl.program_id(1) == 0)
    def _():
        s_ref[...] = jnp.zeros_like(s_ref)

    row = lax.broadcasted_iota(jnp.int32, (C, C), 0)
    col = lax.broadcasted_iota(jnp.int32, (C, C), 1)
    strict = col < row
    eye = (col == row).astype(F32)
    for h in range(GDN_HEADS):
        hs = slice(h * GDN_DK, (h + 1) * GDN_DK)
        k = k_ref[:, hs]
        lmat = l_ref[:, hs]
        kq = _dot_nt(jnp.concatenate([kb_ref[:, hs], q_ref[:, hs]], axis=0), k)
        a = jnp.where(strict, kq[:C] * lmat, 0.0)
        aqk = kq[C:] * lmat
        t = _unit_lower_inverse(a, eye)
        t_hi = t.astype(BF16)
        t_lo = (t - t_hi.astype(F32)).astype(BF16)
        rhs = jnp.concatenate([vb_ref[:, hs], kbd_ref[:, hs]], axis=1)
        uw = _dot(t_hi, rhs) + _dot(t_lo, rhs)
        u = uw[:, :GDN_DV]
        w = uw[:, GDN_DV:]
        state = s_ref[h]
        ws_qs = _dot(jnp.concatenate([w.astype(BF16), qd_ref[:, hs]], axis=0), state.astype(BF16))
        v_new = u - ws_qs[:C]
        vnb = v_new.astype(BF16)
        o = ws_qs[C:] + _dot(aqk.astype(BF16), vnb)
        s_ref[h] = state * cd_ref[:, hs] + _dot_tn(kd_ref[:, hs], vnb)
        o = o * lax.rsqrt(jnp.mean(o * o, axis=-1, keepdims=True) + NORM_EPS) * nw_ref[...]
        o_ref[:, hs] = (o * zg_ref[:, hs].astype(F32)).astype(BF16)


def _gdn_core(ops, norm_w, j):
    q = ops[0]
    B, S, hw = q.shape
    C = GDN_CHUNK
    tile = lambda b, i: (b, i, 0)
    blk = pl.BlockSpec((None, C, hw), tile)
    return pl.pallas_call(
        _gdn_core_kernel,
        grid=(B, S // C),
        in_specs=[blk] * 9 + [pl.BlockSpec((None, None, 1, hw), lambda b, i: (b, i, 0, 0)),
                              _const_spec((None, 1, GDN_DV), lambda b, i: (j, 0, 0))],
        out_specs=blk,
        out_shape=jax.ShapeDtypeStruct((B, S, hw), BF16),
        scratch_shapes=[pltpu.VMEM((GDN_HEADS, GDN_DK, GDN_DV), F32)],
        compiler_params=pltpu.CompilerParams(dimension_semantics=("arbitrary", "arbitrary")),
        name="gdn_core",
    )(*ops, norm_w)


def _pad_lanes(t):
    return jnp.pad(t, [(0, 0)] * (t.ndim - 1) + [(0, V7X_LANES - t.shape[-1])])


def kernel(x, positions, ffn1_norm, ffn1_w_gate_up, ffn1_w_down, mix_norm, ffn2_norm, ffn2_w_gate_up,
           ffn2_w_down, attn_w_in, attn_b_in, attn_sinks, attn_w_out, attn_b_out, gdn_w_in, gdn_conv_w,
           gdn_A_log, gdn_dt_bias, gdn_norm_w, gdn_w_out, final_norm):
    B, S, D = x.shape
    T = B * S
    depth = ffn1_norm.shape[0]
    assert S % ATTN_BLOCK == 0 and S % GDN_CHUNK == 0 and T % min(FFN_TM, T) == 0

    w1gu, w1d = ffn1_w_gate_up.astype(BF16), ffn1_w_down.astype(BF16)
    w2gu, w2d = ffn2_w_gate_up.astype(BF16), ffn2_w_down.astype(BF16)
    n1, nm, n2 = (t[:, None, :] for t in (ffn1_norm, mix_norm, ffn2_norm))
    a_w_in, a_w_out = attn_w_in.astype(BF16), attn_w_out.astype(BF16)
    a_b_in, a_b_out = attn_b_in[:, None, :], attn_b_out[:, None, :]
    conv_cols = gdn_conv_w.shape[-1]
    z_cols = GDN_HEADS * GDN_DV
    g_w_qkvz = gdn_w_in[:, :, :conv_cols + z_cols].astype(BF16)
    g_w_b = _pad_lanes(gdn_w_in[:, :, conv_cols + z_cols:conv_cols + z_cols + GDN_HEADS]).astype(BF16)
    g_w_a = _pad_lanes(gdn_w_in[:, :, conv_cols + z_cols + GDN_HEADS:]).astype(BF16)
    g_alog = _pad_lanes(gdn_A_log)[:, None, :]
    g_dtb = _pad_lanes(gdn_dt_bias)[:, None, :]
    g_nw = gdn_norm_w[:, None, :]
    g_w_out = gdn_w_out.astype(BF16)
    lane_d = jnp.arange(V7X_LANES) % ATTN_HEAD_DIM
    inv_freq = ROPE_THETA ** (-jnp.arange(0, ROPE_DIM, 2, dtype=F32) / ROPE_DIM)
    invf = jnp.where(lane_d < ROPE_DIM, inv_freq[lane_d % (ROPE_DIM // 2)], 0.0)[None, :].astype(F32)
    pos = positions.reshape(T, 1)

    h = x.reshape(T, D)
    proj = None
    for layer in range(depth):
        h = _ffn(h, n1, w1gu, w1d, layer, proj=proj)
        j = layer // 2
        if layer % 2 == 0:
            q, krep, vrep = _attn_qkv(h, pos, invf, nm, a_w_in, a_b_in, layer, j)
            o = _attn_core(q.reshape(B, S, -1), krep.reshape(B, S, -1), vrep.reshape(B, S, -1), attn_sinks[j])
            proj = (o.reshape(T, -1), a_w_out, j, a_b_out)
        else:
            ops = _gdn_in(h.reshape(B, S, D), nm, g_w_qkvz, g_w_b, g_w_a, gdn_conv_w, g_alog, g_dtb, layer, j)
            o = _gdn_core(ops, g_nw, j)
            proj = (o.reshape(T, -1), g_w_out, j, None)
        last = layer == depth - 1
        h = _ffn(h, n2, w2gu, w2d, layer, proj=proj, final_w=final_norm[None, :] if last else None)
        proj = None
    return h.reshape(B, S, D)
```

```python
import functools

import jax
import jax.numpy as jnp
from jax import lax
from jax.experimental import pallas as pl
from jax.experimental.pallas import tpu as pltpu

F32 = jnp.float32
BF16 = jnp.bfloat16

NORM_EPS = 1e-6

ATTN_Q_HEADS = 16
ATTN_KV_HEADS = 4
ATTN_HEAD_DIM = 64
ATTN_GROUP = ATTN_Q_HEADS // ATTN_KV_HEADS
ATTN_BLOCK = 128
ROPE_DIM = ATTN_HEAD_DIM // 4
ROPE_THETA = 500000.0
GDN_HEADS = 8
GDN_DK = 128
GDN_DV = 128
GDN_CONV = 4
GDN_CHUNK = 128

V7X_LANES = 128
V7X_SUBLANES = 8
V7X_VMEM_BYTES = 64 * 1024 * 1024

FFN_TM = 512
FFN_FF_CHUNK = 256
QKV_TM = 512
ATTN_TQ = 256
GDN_IN_TM = 256


def _vmem_limit(estimate_bytes):
    return int(min(V7X_VMEM_BYTES - (4 << 20), max(estimate_bytes, 16 << 20)))


def _dot(a, b):
    return jnp.dot(a, b, preferred_element_type=F32)


def _dot_nt(a, b):
    return lax.dot_general(a, b, (((1,), (1,)), ((), ())), preferred_element_type=F32)


def _dot_tn(a, b):
    return lax.dot_general(a, b, (((0,), (0,)), ((), ())), preferred_element_type=F32)


def _rms(x, w):
    ms = jnp.mean(x * x, axis=-1, keepdims=True)
    return x * lax.rsqrt(ms + NORM_EPS) * w


def _silu(x):
    return x * jax.nn.sigmoid(x)


def _const_spec(shape, index_map):
    return pl.BlockSpec(shape, index_map, pipeline_mode=pl.Buffered(1))


def _ffn_kernel(*refs, d_ff, ff_chunk, has_proj, has_bias, has_final):
    refs = list(refs)
    h_ref = refs.pop(0)
    if has_proj:
        o_ref = refs.pop(0)
        wo_ref = refs.pop(0)
    if has_bias:
        bo_ref = refs.pop(0)
    nw_ref = refs.pop(0)
    wgu_ref = refs.pop(0)
    wd_ref = refs.pop(0)
    if has_final:
        fw_ref = refs.pop(0)
    out_ref, act_ref = refs

    x = h_ref[...]
    if has_proj:
        x = x + _dot(o_ref[...], wo_ref[...])
    if has_bias:
        x = x + bo_ref[...]
    xn = _rms(x, nw_ref[...]).astype(BF16)
    for c in range(d_ff // ff_chunk):
        lo = c * ff_chunk
        g = _dot(xn, wgu_ref[:, lo:lo + ff_chunk])
        u = _dot(xn, wgu_ref[:, d_ff + lo:d_ff + lo + ff_chunk])
        act_ref[:, lo:lo + ff_chunk] = (_silu(g) * u).astype(BF16)
    y = x + 0.5 * _dot(act_ref[...], wd_ref[...])
    if has_final:
        y = _rms(y, fw_ref[...])
    out_ref[...] = y


def _ffn(h, norm_w, wgu, wd, layer, proj=None, final_w=None):
    T, D = h.shape
    d_ff = wd.shape[1]
    tm = min(FFN_TM, T)
    ff_chunk = FFN_FF_CHUNK if d_ff % FFN_FF_CHUNK == 0 else d_ff
    args = [h]
    specs = [pl.BlockSpec((tm, D), lambda i: (i, 0))]
    est = 4 * tm * D * 4 + wgu.shape[1] * wgu.shape[2] * 2 + d_ff * D * 2 + tm * d_ff * 2 + 6 * tm * ff_chunk * 4
    has_proj = proj is not None
    has_bias = has_proj and proj[3] is not None
    if has_proj:
        o, w_out, j, bias = proj
        P = o.shape[1]
        args += [o, w_out]
        specs += [pl.BlockSpec((tm, P), lambda i: (i, 0)),
                  _const_spec((None, P, D), lambda i: (j, 0, 0))]
        est += 2 * tm * P * 2 + P * D * 2
        if has_bias:
            args.append(bias)
            specs.append(_const_spec((None, 1, D), lambda i: (j, 0, 0)))
    args += [norm_w, wgu, wd]
    specs += [_const_spec((None, 1, D), lambda i: (layer, 0, 0)),
              _const_spec((None, D, 2 * d_ff), lambda i: (layer, 0, 0)),
              _const_spec((None, d_ff, D), lambda i: (layer, 0, 0))]
    if final_w is not None:
        args.append(final_w)
        specs.append(_const_spec((1, D), lambda i: (0, 0)))
    kern = functools.partial(_ffn_kernel, d_ff=d_ff, ff_chunk=ff_chunk, has_proj=has_proj,
                             has_bias=has_bias, has_final=final_w is not None)
    return pl.pallas_call(
        kern,
        grid=(T // tm,),
        in_specs=specs,
        out_specs=pl.BlockSpec((tm, D), lambda i: (i, 0)),
        out_shape=jax.ShapeDtypeStruct((T, D), F32),
        scratch_shapes=[pltpu.VMEM((tm, d_ff), BF16)],
        compiler_params=pltpu.CompilerParams(dimension_semantics=("parallel",),
                                             vmem_limit_bytes=_vmem_limit(est + (8 << 20))),
        name="ffn",
    )(*args)


def _attn_qkv_kernel(h_ref, pos_ref, invf_ref, nw_ref, w_ref, b_ref, q_ref, k_ref, v_ref, *, q_w, kv_w):
    xn = _rms(h_ref[...], nw_ref[...]).astype(BF16)
    qkv = _dot(xn, w_ref[...]) + b_ref[...]
    ang = pos_ref[...].astype(F32) * invf_ref[...]
    cos = jnp.cos(ang)
    sin = jnp.sin(ang)
    lane = lax.broadcasted_iota(jnp.int32, (1, V7X_LANES), 1)
    d = lane % ATTN_HEAD_DIM
    first_half = d < (ROPE_DIM // 2)
    sin_signed = jnp.where(first_half, -sin, sin)
    half = ROPE_DIM // 2

    def rope(t):
        up = pltpu.roll(t, V7X_LANES - half, axis=1)
        dn = pltpu.roll(t, half, axis=1)
        return t * cos + jnp.where(first_half, up, dn) * sin_signed

    scale = ATTN_HEAD_DIM ** -0.5
    for s in range(q_w // V7X_LANES):
        sl = slice(s * V7X_LANES, (s + 1) * V7X_LANES)
        q_ref[:, sl] = (rope(qkv[:, sl]) * scale).astype(BF16)
    low = lane < ATTN_HEAD_DIM
    for s in range(kv_w // V7X_LANES):
        ks = rope(qkv[:, q_w + s * V7X_LANES:q_w + (s + 1) * V7X_LANES])
        vs = qkv[:, q_w + kv_w + s * V7X_LANES:q_w + kv_w + (s + 1) * V7X_LANES]
        for t, ref in ((ks, k_ref), (vs, v_ref)):
            sw = pltpu.roll(t, ATTN_HEAD_DIM, axis=1)
            ref[:, (2 * s) * V7X_LANES:(2 * s + 1) * V7X_LANES] = jnp.where(low, t, sw).astype(BF16)
            ref[:, (2 * s + 1) * V7X_LANES:(2 * s + 2) * V7X_LANES] = jnp.where(low, sw, t).astype(BF16)


def _attn_qkv(h, pos, invf, norm_w, w_in, b_in, layer, j):
    T, D = h.shape
    q_w = ATTN_Q_HEADS * ATTN_HEAD_DIM
    kv_w = ATTN_KV_HEADS * ATTN_HEAD_DIM
    n_in = q_w + 2 * kv_w
    rep_w = ATTN_KV_HEADS * V7X_LANES
    tm = min(QKV_TM, T)
    est = 2 * tm * D * 4 + D * n_in * 2 + 3 * tm * n_in * 4 + 2 * tm * (q_w + 2 * rep_w) * 2
    kern = functools.partial(_attn_qkv_kernel, q_w=q_w, kv_w=kv_w)
    return pl.pallas_call(
        kern,
        grid=(T // tm,),
        in_specs=[pl.BlockSpec((tm, D), lambda i: (i, 0)),
                  pl.BlockSpec((tm, 1), lambda i: (i, 0)),
                  _const_spec((1, V7X_LANES), lambda i: (0, 0)),
                  _const_spec((None, 1, D), lambda i: (layer, 0, 0)),
                  _const_spec((None, D, n_in), lambda i: (j, 0, 0)),
                  _const_spec((None, 1, n_in), lambda i: (j, 0, 0))],
        out_specs=[pl.BlockSpec((tm, q_w), lambda i: (i, 0)),
                   pl.BlockSpec((tm, rep_w), lambda i: (i, 0)),
                   pl.BlockSpec((tm, rep_w), lambda i: (i, 0))],
        out_shape=[jax.ShapeDtypeStruct((T, q_w), BF16),
                   jax.ShapeDtypeStruct((T, rep_w), BF16),
                   jax.ShapeDtypeStruct((T, rep_w), BF16)],
        compiler_params=pltpu.CompilerParams(dimension_semantics=("parallel",),
                                             vmem_limit_bytes=_vmem_limit(est + (8 << 20))),
        name="attn_qkv",
    )(h, pos, invf, norm_w, w_in, b_in)


def _attn_core_kernel(sink_ref, q_ref, kc_ref, kp_ref, vc_ref, vp_ref, o_ref, *, tq):
    blk = ATTN_BLOCK
    first_tile = pl.program_id(1) == 0
    row = lax.broadcasted_iota(jnp.int32, (blk, blk), 0)
    col = lax.broadcasted_iota(jnp.int32, (blk, blk), 1)
    tril = col <= row
    lane = lax.broadcasted_iota(jnp.int32, (1, V7X_LANES), 1)
    half_mask = (lane < ATTN_HEAD_DIM, lane >= ATTN_HEAD_DIM)
    no_prev = jnp.where(first_tile, -jnp.inf, 0.0).astype(F32)
    heads = range(ATTN_Q_HEADS)
    pairs = range(ATTN_Q_HEADS // 2)
    for j in range(tq // blk):
        rows = slice(j * blk, (j + 1) * blk)
        prows = slice((j - 1) * blk, j * blk)
        km, vm = [], []
        for h in range(ATTN_KV_HEADS):
            hs = slice(h * V7X_LANES, (h + 1) * V7X_LANES)
            kp = kp_ref[:, hs] if j == 0 else kc_ref[prows, hs]
            vp = vp_ref[:, hs] if j == 0 else vc_ref[prows, hs]
            kband = jnp.concatenate([kp, kc_ref[rows, hs]], axis=0)
            vband = jnp.concatenate([vp, vc_ref[rows, hs]], axis=0)
            km.append([jnp.where(m, kband, jnp.zeros_like(kband)) for m in half_mask])
            vm.append([jnp.where(m, vband, jnp.zeros_like(vband)) for m in half_mask])
        qsl = [slice(p * V7X_LANES, (p + 1) * V7X_LANES) for p in pairs]
        s = [_dot_nt(q_ref[rows, qsl[n // 2]], km[n // ATTN_GROUP][n % 2]) for n in heads]
        pband = []
        for n in heads:
            s_prev = s[n][:, :blk]
            if j == 0:
                s_prev = s_prev + no_prev
            sc = jnp.where(tril, s[n][:, blk:], s_prev)
            sink = sink_ref[n]
            m = jnp.maximum(jnp.max(sc, axis=-1, keepdims=True), sink)
            pr = jnp.exp(sc - m)
            denom = jnp.sum(pr, axis=-1, keepdims=True) + jnp.exp(sink - m)
            pn = pr * (1.0 / denom)
            zero = jnp.zeros_like(pn)
            pband.append(jnp.concatenate([jnp.where(tril, zero, pn), jnp.where(tril, pn, zero)],
                                         axis=1).astype(BF16))
        pv = [_dot(pband[n], vm[n // ATTN_GROUP][n % 2]) for n in heads]
        for p in pairs:
            o_ref[rows, qsl[p]] = (pv[2 * p] + pv[2 * p + 1]).astype(BF16)


def _attn_core(q, krep, vrep, sinks):
    B, S, q_w = q.shape
    rep_w = krep.shape[-1]
    tq = min(ATTN_TQ, S)
    nb = tq // ATTN_BLOCK
    kern = functools.partial(_attn_core_kernel, tq=tq)
    cur = lambda b, i: (b, i, 0)
    prev = lambda b, i: (b, jnp.maximum(i * nb - 1, 0), 0)
    return pl.pallas_call(
        kern,
        grid=(B, S // tq),
        in_specs=[pl.BlockSpec(memory_space=pltpu.SMEM),
                  pl.BlockSpec((None, tq, q_w), cur),
                  pl.BlockSpec((None, tq, rep_w), cur),
                  pl.BlockSpec((None, ATTN_BLOCK, rep_w), prev),
                  pl.BlockSpec((None, tq, rep_w), cur),
                  pl.BlockSpec((None, ATTN_BLOCK, rep_w), prev)],
        out_specs=pl.BlockSpec((None, tq, q_w), cur),
        out_shape=jax.ShapeDtypeStruct((B, S, q_w), BF16),
        compiler_params=pltpu.CompilerParams(dimension_semantics=("parallel", "parallel")),
        name="attn_core",
    )(sinks, q, krep, krep, vrep, vrep)


def _split3(x):
    a = x.astype(BF16)
    r = x - a.astype(F32)
    b = r.astype(BF16)
    c = (r - b.astype(F32)).astype(BF16)
    return a, b, c


def _gdn_in_kernel(h_ref, nw_ref, w_ref, wb_ref, wa_ref, cw_ref, alog_ref, dtb_ref,
                   q_ref, k_ref, kb_ref, vb_ref, kbd_ref, qd_ref, kd_ref, zg_ref, l_ref, cd_ref,
                   buf_ref, *, tm):
    C = GDN_CHUNK
    qk_w = GDN_HEADS * GDN_DK
    conv_w = 2 * qk_w + GDN_HEADS * GDN_DV
    halo = V7X_SUBLANES

    @pl.when(pl.program_id(1) == 0)
    def _():
        buf_ref[0:halo, :] = jnp.zeros((halo, conv_w), F32)

    xn = _rms(h_ref[...], nw_ref[...]).astype(BF16)
    buf_ref[halo:halo + tm, :] = _dot(xn, w_ref[:, :conv_w])
    zg_ref[...] = _silu(_dot(xn, w_ref[:, conv_w:])).astype(BF16)
    beta = jax.nn.sigmoid(_dot(xn, wb_ref[...]))
    a = _dot(xn, wa_ref[...])
    g = -jnp.exp(alog_ref[...]) * jax.nn.softplus(a + dtb_ref[...])

    row = lax.broadcasted_iota(jnp.int32, (C, C), 0)
    col = lax.broadcasted_iota(jnp.int32, (C, C), 1)
    causal = col <= row
    tril = causal.astype(BF16)

    def conv_silu(lo):
        acc = None
        for i in range(GDN_CONV):
            off = halo - (GDN_CONV - 1) + i
            term = cw_ref[i:i + 1, lo:lo + V7X_LANES] * buf_ref[off:off + tm, lo:lo + V7X_LANES]
            acc = term if acc is None else acc + term
        return _silu(acc)

    def l2n(t):
        return t * lax.rsqrt(jnp.sum(t * t, axis=-1, keepdims=True) + NORM_EPS)

    d_cols, d_rows = [], []
    for c in range(tm // C):
        g1, g2, g3 = _split3(g[c * C:(c + 1) * C, :])
        dc = _dot(tril, g1) + _dot(tril, g2) + _dot(tril, g3)
        d_cols.append(dc)
        d_rows.append(dc.T)

    for h in range(GDN_HEADS):
        hs = slice(h * GDN_DK, (h + 1) * GDN_DK)
        qh = l2n(conv_silu(h * GDN_DK)) * (GDN_DK ** -0.5)
        kh = l2n(conv_silu(qk_w + h * GDN_DK))
        vh = conv_silu(2 * qk_w + h * GDN_DV)
        for c in range(tm // C):
            rs = slice(c * C, (c + 1) * C)
            dB = jnp.broadcast_to(d_cols[c][:, h:h + 1], (C, C))
            bB = jnp.broadcast_to(beta[rs, h:h + 1], (C, C))
            dR = jnp.broadcast_to(d_rows[c][h:h + 1, :], (C, C))
            dL = jnp.broadcast_to(dB[C - 1:C, :], (C, C))
            eB = jnp.exp(dB)
            q_c, k_c, v_c = qh[rs], kh[rs], vh[rs]
            kb = k_c * bB
            q_ref[rs, hs] = q_c.astype(BF16)
            k_ref[rs, hs] = k_c.astype(BF16)
            kb_ref[rs, hs] = kb.astype(BF16)
            vb_ref[rs, hs] = (v_c * bB).astype(BF16)
            kbd_ref[rs, hs] = (kb * eB).astype(BF16)
            qd_ref[rs, hs] = (q_c * eB).astype(BF16)
            kd_ref[rs, hs] = (k_c * jnp.exp(dL - dB)).astype(BF16)
            l_ref[rs, hs] = jnp.exp(jnp.where(causal, dB - dR, -jnp.inf))
            cd_ref[c, :, hs] = jnp.exp(dL[0:1, :])

    buf_ref[0:halo, :] = buf_ref[tm:tm + halo, :]


def _gdn_in(h, norm_w, w_qkvz, w_b, w_a, conv_w, alog, dtb, layer, j):
    B, S, D = h.shape
    hw = GDN_HEADS * GDN_DK
    conv_cols = conv_w.shape[-1]
    tm = min(GDN_IN_TM, S)
    n_chunks = S // GDN_CHUNK
    tile = lambda b, i: (b, i, 0)
    big = pl.BlockSpec((None, tm, hw), tile)
    est = (2 * tm * D * 4 + D * w_qkvz.shape[-1] * 2 + 2 * D * V7X_LANES * 2 + (tm + 8) * conv_cols * 4
           + 2 * 8 * tm * hw * 2 + 2 * tm * hw * 4 + 3 * tm * hw * 4)
    kern = functools.partial(_gdn_in_kernel, tm=tm)
    outs = pl.pallas_call(
        kern,
        grid=(B, S // tm),
        in_specs=[pl.BlockSpec((None, tm, D), tile),
                  _const_spec((None, 1, D), lambda b, i: (layer, 0, 0)),
                  _const_spec((None, D, w_qkvz.shape[-1]), lambda b, i: (j, 0, 0)),
                  _const_spec((None, D, V7X_LANES), lambda b, i: (j, 0, 0)),
                  _const_spec((None, D, V7X_LANES), lambda b, i: (j, 0, 0)),
                  _const_spec((None, GDN_CONV, conv_cols), lambda b, i: (j, 0, 0)),
                  _const_spec((None, 1, V7X_LANES), lambda b, i: (j, 0, 0)),
                  _const_spec((None, 1, V7X_LANES), lambda b, i: (j, 0, 0))],
        out_specs=[big] * 8 + [pl.BlockSpec((None, tm, hw), tile),
                               pl.BlockSpec((None, tm // GDN_CHUNK, 1, hw), lambda b, i: (b, i, 0, 0))],
        out_shape=[jax.ShapeDtypeStruct((B, S, hw), BF16)] * 8
                  + [jax.ShapeDtypeStruct((B, S, hw), F32),
                     jax.ShapeDtypeStruct((B, n_chunks, 1, hw), F32)],
        scratch_shapes=[pltpu.VMEM((tm + V7X_SUBLANES, conv_cols), F32)],
        compiler_params=pltpu.CompilerParams(dimension_semantics=("arbitrary", "arbitrary"),
                                             vmem_limit_bytes=_vmem_limit(est + (8 << 20))),
        name="gdn_in",
    )(h, norm_w, w_qkvz, w_b, w_a, conv_w, alog, dtb)
    return outs


def _hi_lo(x):
    hi = x.astype(BF16)
    return hi, (x - hi.astype(F32)).astype(BF16)


def _unit_lower_inverse(a_list, eye):
    n = range(len(a_list))
    C = a_list[0].shape[0]
    p = [eye - a for a in a_list]
    x = [a.astype(BF16) for a in a_list]
    for _ in range(C.bit_length() - 2):
        x = [_dot(x[i], x[i]).astype(BF16) for i in n]
        p = [p[i] + _dot(p[i].astype(BF16), x[i]) for i in n]
    a_hl = [_hi_lo(a) for a in a_list]
    p_hl = [_hi_lo(t) for t in p]
    resid = [eye - p[i] - (_dot(a_hl[i][0], p_hl[i][0]) + _dot(a_hl[i][0], p_hl[i][1])
                           + _dot(a_hl[i][1], p_hl[i][0])) for i in n]
    return [p[i] + _dot(p_hl[i][0], resid[i].astype(BF16)) for i in n]


def _gdn_core_kernel(q_ref, k_ref, kb_ref, vb_ref, kbd_ref, qd_ref, kd_ref, zg_ref, l_ref, cd_ref, nw_ref,
                     o_ref, s_ref):
    C = GDN_CHUNK

    @pl.when(pl.program_id(1) == 0)
    def _():
        s_ref[...] = jnp.zeros_like(s_ref)

    row = lax.broadcasted_iota(jnp.int32, (C, C), 0)
    col = lax.broadcasted_iota(jnp.int32, (C, C), 1)
    strict = col < row
    eye = (col == row).astype(F32)
    heads = range(GDN_HEADS)
    hs = [slice(h * GDN_DK, (h + 1) * GDN_DK) for h in heads]
    kq = [_dot_nt(jnp.concatenate([kb_ref[:, hs[h]], q_ref[:, hs[h]]], axis=0), k_ref[:, hs[h]]) for h in heads]
    a = [jnp.where(strict, kq[h][:C] * l_ref[:, hs[h]], 0.0) for h in heads]
    aqk = [(kq[h][C:] * l_ref[:, hs[h]]).astype(BF16) for h in heads]
    t = [_hi_lo(t_h) for t_h in _unit_lower_inverse(a, eye)]
    uw = []
    for h in heads:
        rhs = jnp.concatenate([vb_ref[:, hs[h]], kbd_ref[:, hs[h]]], axis=1)
        uw.append(_dot(t[h][0], rhs) + _dot(t[h][1], rhs))
    state = [s_ref[h] for h in heads]
    ws_qs = [_dot(jnp.concatenate([uw[h][:, GDN_DV:].astype(BF16), qd_ref[:, hs[h]]], axis=0),
                  state[h].astype(BF16)) for h in heads]
    vnb = [(uw[h][:, :GDN_DV] - ws_qs[h][:C]).astype(BF16) for h in heads]
    o = [ws_qs[h][C:] + _dot(aqk[h], vnb[h]) for h in heads]
    new_state = [state[h] * cd_ref[:, hs[h]] + _dot_tn(kd_ref[:, hs[h]], vnb[h]) for h in heads]
    for h in heads:
        s_ref[h] = new_state[h]
        on = o[h] * lax.rsqrt(jnp.mean(o[h] * o[h], axis=-1, keepdims=True) + NORM_EPS) * nw_ref[...]
        o_ref[:, hs[h]] = (on * zg_ref[:, hs[h]].astype(F32)).astype(BF16)


def _gdn_core(ops, norm_w, j):
    q = ops[0]
    B, S, hw = q.shape
    C = GDN_CHUNK
    tile = lambda b, i: (b, i, 0)
    blk = pl.BlockSpec((None, C, hw), tile)
    return pl.pallas_call(
        _gdn_core_kernel,
        grid=(B, S // C),
        in_specs=[blk] * 9 + [pl.BlockSpec((None, None, 1, hw), lambda b, i: (b, i, 0, 0)),
                              _const_spec((None, 1, GDN_DV), lambda b, i: (j, 0, 0))],
        out_specs=blk,
        out_shape=jax.ShapeDtypeStruct((B, S, hw), BF16),
        scratch_shapes=[pltpu.VMEM((GDN_HEADS, GDN_DK, GDN_DV), F32)],
        compiler_params=pltpu.CompilerParams(dimension_semantics=("arbitrary", "arbitrary")),
        name="gdn_core",
    )(*ops, norm_w)


def _pad_lanes(t):
    return jnp.pad(t, [(0, 0)] * (t.ndim - 1) + [(0, V7X_LANES - t.shape[-1])])


def kernel(x, positions, ffn1_norm, ffn1_w_gate_up, ffn1_w_down, mix_norm, ffn2_norm, ffn2_w_gate_up,
           ffn2_w_down, attn_w_in, attn_b_in, attn_sinks, attn_w_out, attn_b_out, gdn_w_in, gdn_conv_w,
           gdn_A_log, gdn_dt_bias, gdn_norm_w, gdn_w_out, final_norm):
    B, S, D = x.shape
    T = B * S
    depth = ffn1_norm.shape[0]
    assert S % ATTN_BLOCK == 0 and S % GDN_CHUNK == 0 and T % min(FFN_TM, T) == 0

    w1gu, w1d = ffn1_w_gate_up.astype(BF16), ffn1_w_down.astype(BF16)
    w2gu, w2d = ffn2_w_gate_up.astype(BF16), ffn2_w_down.astype(BF16)
    n1, nm, n2 = (t[:, None, :] for t in (ffn1_norm, mix_norm, ffn2_norm))
    a_w_in, a_w_out = attn_w_in.astype(BF16), attn_w_out.astype(BF16)
    a_b_in, a_b_out = attn_b_in[:, None, :], attn_b_out[:, None, :]
    conv_cols = gdn_conv_w.shape[-1]
    z_cols = GDN_HEADS * GDN_DV
    g_w_qkvz = gdn_w_in[:, :, :conv_cols + z_cols].astype(BF16)
    g_w_b = _pad_lanes(gdn_w_in[:, :, conv_cols + z_cols:conv_cols + z_cols + GDN_HEADS]).astype(BF16)
    g_w_a = _pad_lanes(gdn_w_in[:, :, conv_cols + z_cols + GDN_HEADS:]).astype(BF16)
    g_alog = _pad_lanes(gdn_A_log)[:, None, :]
    g_dtb = _pad_lanes(gdn_dt_bias)[:, None, :]
    g_nw = gdn_norm_w[:, None, :]
    g_w_out = gdn_w_out.astype(BF16)
    lane_d = jnp.arange(V7X_LANES) % ATTN_HEAD_DIM
    inv_freq = ROPE_THETA ** (-jnp.arange(0, ROPE_DIM, 2, dtype=F32) / ROPE_DIM)
    invf = jnp.where(lane_d < ROPE_DIM, inv_freq[lane_d % (ROPE_DIM // 2)], 0.0)[None, :].astype(F32)
    pos = positions.reshape(T, 1)

    h = x.reshape(T, D)
    proj = None
    for layer in range(depth):
        h = _ffn(h, n1, w1gu, w1d, layer, proj=proj)
        j = layer // 2
        if layer % 2 == 0:
            q, krep, vrep = _attn_qkv(h, pos, invf, nm, a_w_in, a_b_in, layer, j)
            o = _attn_core(q.reshape(B, S, -1), krep.reshape(B, S, -1), vrep.reshape(B, S, -1), attn_sinks[j])
            proj = (o.reshape(T, -1), a_w_out, j, a_b_out)
        else:
            ops = _gdn_in(h.reshape(B, S, D), nm, g_w_qkvz, g_w_b, g_w_a, gdn_conv_w, g_alog, g_dtb, layer, j)
            o = _gdn_core(ops, g_nw, j)
            proj = (o.reshape(T, -1), g_w_out, j, None)
        last = layer == depth - 1
        h = _ffn(h, n2, w2gu, w2d, layer, proj=proj, final_w=final_norm[None, :] if last else None)
        proj = None
    return h.reshape(B, S, D)
```

```python
import functools

import jax
import jax.numpy as jnp
from jax import lax
from jax.experimental import pallas as pl
from jax.experimental.pallas import tpu as pltpu

F32 = jnp.float32
BF16 = jnp.bfloat16

NORM_EPS = 1e-6

ATTN_Q_HEADS = 16
ATTN_KV_HEADS = 4
ATTN_HEAD_DIM = 64
ATTN_GROUP = ATTN_Q_HEADS // ATTN_KV_HEADS
ATTN_BLOCK = 128
ROPE_DIM = ATTN_HEAD_DIM // 4
ROPE_THETA = 500000.0
GDN_HEADS = 8
GDN_DK = 128
GDN_DV = 128
GDN_CONV = 4
GDN_CHUNK = 128

V7X_LANES = 128
V7X_SUBLANES = 8
V7X_VMEM_BYTES = 64 * 1024 * 1024

FFN_TM = 512
FFN_FF_CHUNK = 256
QKV_TM = 512
ATTN_TQ = 256
GDN_IN_TM = 256


def _vmem_limit(estimate_bytes):
    return int(min(V7X_VMEM_BYTES - (4 << 20), max(estimate_bytes, 16 << 20)))


def _dot(a, b):
    return jnp.dot(a, b, preferred_element_type=F32)


def _dot_nt(a, b):
    return lax.dot_general(a, b, (((1,), (1,)), ((), ())), preferred_element_type=F32)


def _dot_tn(a, b):
    return lax.dot_general(a, b, (((0,), (0,)), ((), ())), preferred_element_type=F32)


def _rms(x, w):
    ms = jnp.mean(x * x, axis=-1, keepdims=True)
    return x * lax.rsqrt(ms + NORM_EPS) * w


def _silu(x):
    return (0.5 * x) * (1.0 + jnp.tanh(0.5 * x))


def _const_spec(shape, index_map):
    return pl.BlockSpec(shape, index_map, pipeline_mode=pl.Buffered(1))


def _ffn_kernel(*refs, d_ff, ff_chunk, has_proj, has_bias, has_final):
    refs = list(refs)
    h_ref = refs.pop(0)
    if has_proj:
        o_ref = refs.pop(0)
        wo_ref = refs.pop(0)
    if has_bias:
        bo_ref = refs.pop(0)
    nw_ref = refs.pop(0)
    wgu_ref = refs.pop(0)
    wd_ref = refs.pop(0)
    if has_final:
        fw_ref = refs.pop(0)
    out_ref, act_ref = refs

    x = h_ref[...]
    if has_proj:
        x = x + _dot(o_ref[...], wo_ref[...])
    if has_bias:
        x = x + bo_ref[...]
    xn = _rms(x, nw_ref[...]).astype(BF16)
    for c in range(d_ff // ff_chunk):
        lo = c * ff_chunk
        g = _dot(xn, wgu_ref[:, lo:lo + ff_chunk])
        u = _dot(xn, wgu_ref[:, d_ff + lo:d_ff + lo + ff_chunk])
        act_ref[:, lo:lo + ff_chunk] = (_silu(g) * u).astype(BF16)
    y = x + 0.5 * _dot(act_ref[...], wd_ref[...])
    if has_final:
        y = _rms(y, fw_ref[...])
    out_ref[...] = y


def _ffn(h, norm_w, wgu, wd, layer, proj=None, final_w=None):
    T, D = h.shape
    d_ff = wd.shape[1]
    tm = min(FFN_TM, T)
    ff_chunk = FFN_FF_CHUNK if d_ff % FFN_FF_CHUNK == 0 else d_ff
    args = [h]
    specs = [pl.BlockSpec((tm, D), lambda i: (i, 0))]
    est = 4 * tm * D * 4 + wgu.shape[1] * wgu.shape[2] * 2 + d_ff * D * 2 + tm * d_ff * 2 + 6 * tm * ff_chunk * 4
    has_proj = proj is not None
    has_bias = has_proj and proj[3] is not None
    if has_proj:
        o, w_out, j, bias = proj
        P = o.shape[1]
        args += [o, w_out]
        specs += [pl.BlockSpec((tm, P), lambda i: (i, 0)),
                  _const_spec((None, P, D), lambda i: (j, 0, 0))]
        est += 2 * tm * P * 2 + P * D * 2
        if has_bias:
            args.append(bias)
            specs.append(_const_spec((None, 1, D), lambda i: (j, 0, 0)))
    args += [norm_w, wgu, wd]
    specs += [_const_spec((None, 1, D), lambda i: (layer, 0, 0)),
              _const_spec((None, D, 2 * d_ff), lambda i: (layer, 0, 0)),
              _const_spec((None, d_ff, D), lambda i: (layer, 0, 0))]
    if final_w is not None:
        args.append(final_w)
        specs.append(_const_spec((1, D), lambda i: (0, 0)))
    kern = functools.partial(_ffn_kernel, d_ff=d_ff, ff_chunk=ff_chunk, has_proj=has_proj,
                             has_bias=has_bias, has_final=final_w is not None)
    return pl.pallas_call(
        kern,
        grid=(T // tm,),
        in_specs=specs,
        out_specs=pl.BlockSpec((tm, D), lambda i: (i, 0)),
        out_shape=jax.ShapeDtypeStruct((T, D), F32),
        scratch_shapes=[pltpu.VMEM((tm, d_ff), BF16)],
        compiler_params=pltpu.CompilerParams(dimension_semantics=("parallel",),
                                             vmem_limit_bytes=_vmem_limit(est + (8 << 20))),
        name="ffn",
    )(*args)


def _attn_qkv_kernel(h_ref, pos_ref, invf_ref, nw_ref, w_ref, b_ref, q_ref, k_ref, v_ref, *, q_w, kv_w):
    xn = _rms(h_ref[...], nw_ref[...]).astype(BF16)
    qkv = _dot(xn, w_ref[...]) + b_ref[...]
    ang = pos_ref[...].astype(F32) * invf_ref[...]
    cos = jnp.cos(ang)
    sin = jnp.sin(ang)
    lane = lax.broadcasted_iota(jnp.int32, (1, V7X_LANES), 1)
    d = lane % ATTN_HEAD_DIM
    first_half = d < (ROPE_DIM // 2)
    sin_signed = jnp.where(first_half, -sin, sin)
    half = ROPE_DIM // 2

    def rope(t):
        up = pltpu.roll(t, V7X_LANES - half, axis=1)
        dn = pltpu.roll(t, half, axis=1)
        return t * cos + jnp.where(first_half, up, dn) * sin_signed

    scale = ATTN_HEAD_DIM ** -0.5
    for s in range(q_w // V7X_LANES):
        sl = slice(s * V7X_LANES, (s + 1) * V7X_LANES)
        q_ref[:, sl] = (rope(qkv[:, sl]) * scale).astype(BF16)
    low = lane < ATTN_HEAD_DIM
    for s in range(kv_w // V7X_LANES):
        ks = rope(qkv[:, q_w + s * V7X_LANES:q_w + (s + 1) * V7X_LANES])
        vs = qkv[:, q_w + kv_w + s * V7X_LANES:q_w + kv_w + (s + 1) * V7X_LANES]
        for t, ref in ((ks, k_ref), (vs, v_ref)):
            sw = pltpu.roll(t, ATTN_HEAD_DIM, axis=1)
            ref[:, (2 * s) * V7X_LANES:(2 * s + 1) * V7X_LANES] = jnp.where(low, t, sw).astype(BF16)
            ref[:, (2 * s + 1) * V7X_LANES:(2 * s + 2) * V7X_LANES] = jnp.where(low, sw, t).astype(BF16)


def _attn_qkv(h, pos, invf, norm_w, w_in, b_in, layer, j):
    T, D = h.shape
    q_w = ATTN_Q_HEADS * ATTN_HEAD_DIM
    kv_w = ATTN_KV_HEADS * ATTN_HEAD_DIM
    n_in = q_w + 2 * kv_w
    rep_w = ATTN_KV_HEADS * V7X_LANES
    tm = min(QKV_TM, T)
    est = 2 * tm * D * 4 + D * n_in * 2 + 3 * tm * n_in * 4 + 2 * tm * (q_w + 2 * rep_w) * 2
    kern = functools.partial(_attn_qkv_kernel, q_w=q_w, kv_w=kv_w)
    return pl.pallas_call(
        kern,
        grid=(T // tm,),
        in_specs=[pl.BlockSpec((tm, D), lambda i: (i, 0)),
                  pl.BlockSpec((tm, 1), lambda i: (i, 0)),
                  _const_spec((1, V7X_LANES), lambda i: (0, 0)),
                  _const_spec((None, 1, D), lambda i: (layer, 0, 0)),
                  _const_spec((None, D, n_in), lambda i: (j, 0, 0)),
                  _const_spec((None, 1, n_in), lambda i: (j, 0, 0))],
        out_specs=[pl.BlockSpec((tm, q_w), lambda i: (i, 0)),
                   pl.BlockSpec((tm, rep_w), lambda i: (i, 0)),
                   pl.BlockSpec((tm, rep_w), lambda i: (i, 0))],
        out_shape=[jax.ShapeDtypeStruct((T, q_w), BF16),
                   jax.ShapeDtypeStruct((T, rep_w), BF16),
                   jax.ShapeDtypeStruct((T, rep_w), BF16)],
        compiler_params=pltpu.CompilerParams(dimension_semantics=("parallel",),
                                             vmem_limit_bytes=_vmem_limit(est + (8 << 20))),
        name="attn_qkv",
    )(h, pos, invf, norm_w, w_in, b_in)


def _attn_core_kernel(sink_ref, q_ref, kc_ref, kp_ref, vc_ref, vp_ref, o_ref, *, tq):
    blk = ATTN_BLOCK
    first_tile = pl.program_id(1) == 0
    row = lax.broadcasted_iota(jnp.int32, (blk, blk), 0)
    col = lax.broadcasted_iota(jnp.int32, (blk, blk), 1)
    tril = col <= row
    lane = lax.broadcasted_iota(jnp.int32, (1, V7X_LANES), 1)
    half_mask = (lane < ATTN_HEAD_DIM, lane >= ATTN_HEAD_DIM)
    no_prev = jnp.where(first_tile, -jnp.inf, 0.0).astype(F32)
    heads = range(ATTN_Q_HEADS)
    pairs = range(ATTN_Q_HEADS // 2)
    for j in range(tq // blk):
        rows = slice(j * blk, (j + 1) * blk)
        prows = slice((j - 1) * blk, j * blk)
        km, vm = [], []
        for h in range(ATTN_KV_HEADS):
            hs = slice(h * V7X_LANES, (h + 1) * V7X_LANES)
            kp = kp_ref[:, hs] if j == 0 else kc_ref[prows, hs]
            vp = vp_ref[:, hs] if j == 0 else vc_ref[prows, hs]
            kband = jnp.concatenate([kp, kc_ref[rows, hs]], axis=0)
            vband = jnp.concatenate([vp, vc_ref[rows, hs]], axis=0)
            km.append([jnp.where(m, kband, jnp.zeros_like(kband)) for m in half_mask])
            vm.append([jnp.where(m, vband, jnp.zeros_like(vband)) for m in half_mask])
        qsl = [slice(p * V7X_LANES, (p + 1) * V7X_LANES) for p in pairs]
        s = [_dot_nt(q_ref[rows, qsl[n // 2]], km[n // ATTN_GROUP][n % 2]) for n in heads]
        pband = []
        for n in heads:
            s_prev = s[n][:, :blk]
            if j == 0:
                s_prev = s_prev + no_prev
            sc = jnp.where(tril, s[n][:, blk:], s_prev)
            sink = sink_ref[n]
            m = jnp.maximum(jnp.max(sc, axis=-1, keepdims=True), sink)
            pr = jnp.exp(sc - m)
            denom = jnp.sum(pr, axis=-1, keepdims=True) + jnp.exp(sink - m)
            pn = pr * (1.0 / denom)
            zero = jnp.zeros_like(pn)
            pband.append(jnp.concatenate([jnp.where(tril, zero, pn), jnp.where(tril, pn, zero)],
                                         axis=1).astype(BF16))
        pv = [_dot(pband[n], vm[n // ATTN_GROUP][n % 2]) for n in heads]
        for p in pairs:
            o_ref[rows, qsl[p]] = (pv[2 * p] + pv[2 * p + 1]).astype(BF16)


def _attn_core(q, krep, vrep, sinks):
    B, S, q_w = q.shape
    rep_w = krep.shape[-1]
    tq = min(ATTN_TQ, S)
    nb = tq // ATTN_BLOCK
    kern = functools.partial(_attn_core_kernel, tq=tq)
    cur = lambda b, i: (b, i, 0)
    prev = lambda b, i: (b, jnp.maximum(i * nb - 1, 0), 0)
    return pl.pallas_call(
        kern,
        grid=(B, S // tq),
        in_specs=[pl.BlockSpec(memory_space=pltpu.SMEM),
                  pl.BlockSpec((None, tq, q_w), cur),
                  pl.BlockSpec((None, tq, rep_w), cur),
                  pl.BlockSpec((None, ATTN_BLOCK, rep_w), prev),
                  pl.BlockSpec((None, tq, rep_w), cur),
                  pl.BlockSpec((None, ATTN_BLOCK, rep_w), prev)],
        out_specs=pl.BlockSpec((None, tq, q_w), cur),
        out_shape=jax.ShapeDtypeStruct((B, S, q_w), BF16),
        compiler_params=pltpu.CompilerParams(dimension_semantics=("parallel", "parallel")),
        name="attn_core",
    )(sinks, q, krep, krep, vrep, vrep)


def _split3(x):
    a = x.astype(BF16)
    r = x - a.astype(F32)
    b = r.astype(BF16)
    c = (r - b.astype(F32)).astype(BF16)
    return a, b, c


def _gdn_in_kernel(h_ref, nw_ref, w_ref, wb_ref, wa_ref, cw_ref, alog_ref, dtb_ref,
                   q_ref, k_ref, kb_ref, vb_ref, kbd_ref, qd_ref, kd_ref, zg_ref, l_ref, cd_ref,
                   buf_ref, *, tm):
    C = GDN_CHUNK
    qk_w = GDN_HEADS * GDN_DK
    conv_w = 2 * qk_w + GDN_HEADS * GDN_DV
    halo = V7X_SUBLANES

    @pl.when(pl.program_id(1) == 0)
    def _():
        buf_ref[0:halo, :] = jnp.zeros((halo, conv_w), F32)

    xn = _rms(h_ref[...], nw_ref[...]).astype(BF16)
    buf_ref[halo:halo + tm, :] = _dot(xn, w_ref[:, :conv_w])
    zg_ref[...] = _silu(_dot(xn, w_ref[:, conv_w:])).astype(BF16)
    beta = jax.nn.sigmoid(_dot(xn, wb_ref[...]))
    a = _dot(xn, wa_ref[...])
    g = -jnp.exp(alog_ref[...]) * jax.nn.softplus(a + dtb_ref[...])

    row = lax.broadcasted_iota(jnp.int32, (C, C), 0)
    col = lax.broadcasted_iota(jnp.int32, (C, C), 1)
    causal = col <= row
    tril = causal.astype(BF16)

    def conv_silu(lo):
        xf = buf_ref[:, lo:lo + V7X_LANES]
        acc = cw_ref[GDN_CONV - 1:GDN_CONV, lo:lo + V7X_LANES] * xf
        for s in range(1, GDN_CONV):
            acc = acc + cw_ref[GDN_CONV - 1 - s:GDN_CONV - s, lo:lo + V7X_LANES] * pltpu.roll(xf, s, axis=0)
        return _silu(acc[halo:halo + tm])

    def l2n(t):
        return t * lax.rsqrt(jnp.sum(t * t, axis=-1, keepdims=True) + NORM_EPS)

    d_cols, d_rows, e_cols, kd_cols = [], [], [], []
    for c in range(tm // C):
        g1, g2, g3 = _split3(g[c * C:(c + 1) * C, :])
        dc = _dot(tril, g1) + _dot(tril, g2) + _dot(tril, g3)
        d_cols.append(dc)
        d_rows.append(dc.T)
        e_cols.append(jnp.exp(dc))
        kd_cols.append(jnp.exp(dc[C - 1:C, :] - dc))

    for h in range(GDN_HEADS):
        hs = slice(h * GDN_DK, (h + 1) * GDN_DK)
        qh = l2n(conv_silu(h * GDN_DK)) * (GDN_DK ** -0.5)
        kh = l2n(conv_silu(qk_w + h * GDN_DK))
        vh = conv_silu(2 * qk_w + h * GDN_DV)
        for c in range(tm // C):
            rs = slice(c * C, (c + 1) * C)
            dB = jnp.broadcast_to(d_cols[c][:, h:h + 1], (C, C))
            bB = jnp.broadcast_to(beta[rs, h:h + 1], (C, C))
            eB = jnp.broadcast_to(e_cols[c][:, h:h + 1], (C, C))
            kdB = jnp.broadcast_to(kd_cols[c][:, h:h + 1], (C, C))
            dR = jnp.broadcast_to(d_rows[c][h:h + 1, :], (C, C))
            q_c, k_c, v_c = qh[rs], kh[rs], vh[rs]
            kb = k_c * bB
            q_ref[rs, hs] = q_c.astype(BF16)
            k_ref[rs, hs] = k_c.astype(BF16)
            kb_ref[rs, hs] = kb.astype(BF16)
            vb_ref[rs, hs] = (v_c * bB).astype(BF16)
            kbd_ref[rs, hs] = (kb * eB).astype(BF16)
            qd_ref[rs, hs] = (q_c * eB).astype(BF16)
            kd_ref[rs, hs] = (k_c * kdB).astype(BF16)
            l_ref[rs, hs] = jnp.exp(jnp.where(causal, dB - dR, -jnp.inf))
            cd_ref[c, :, hs] = eB[C - 1:C, :]

    buf_ref[0:halo, :] = buf_ref[tm:tm + halo, :]


def _gdn_in(h, norm_w, w_qkvz, w_b, w_a, conv_w, alog, dtb, layer, j):
    B, S, D = h.shape
    hw = GDN_HEADS * GDN_DK
    conv_cols = conv_w.shape[-1]
    tm = min(GDN_IN_TM, S)
    n_chunks = S // GDN_CHUNK
    tile = lambda b, i: (b, i, 0)
    big = pl.BlockSpec((None, tm, hw), tile)
    est = (2 * tm * D * 4 + D * w_qkvz.shape[-1] * 2 + 2 * D * V7X_LANES * 2 + (tm + 8) * conv_cols * 4
           + 2 * 8 * tm * hw * 2 + 2 * tm * hw * 4 + 3 * tm * hw * 4)
    kern = functools.partial(_gdn_in_kernel, tm=tm)
    outs = pl.pallas_call(
        kern,
        grid=(B, S // tm),
        in_specs=[pl.BlockSpec((None, tm, D), tile),
                  _const_spec((None, 1, D), lambda b, i: (layer, 0, 0)),
                  _const_spec((None, D, w_qkvz.shape[-1]), lambda b, i: (j, 0, 0)),
                  _const_spec((None, D, V7X_LANES), lambda b, i: (j, 0, 0)),
                  _const_spec((None, D, V7X_LANES), lambda b, i: (j, 0, 0)),
                  _const_spec((None, GDN_CONV, conv_cols), lambda b, i: (j, 0, 0)),
                  _const_spec((None, 1, V7X_LANES), lambda b, i: (j, 0, 0)),
                  _const_spec((None, 1, V7X_LANES), lambda b, i: (j, 0, 0))],
        out_specs=[big] * 8 + [pl.BlockSpec((None, tm, hw), tile),
                               pl.BlockSpec((None, tm // GDN_CHUNK, 1, hw), lambda b, i: (b, i, 0, 0))],
        out_shape=[jax.ShapeDtypeStruct((B, S, hw), BF16)] * 8
                  + [jax.ShapeDtypeStruct((B, S, hw), F32),
                     jax.ShapeDtypeStruct((B, n_chunks, 1, hw), F32)],
        scratch_shapes=[pltpu.VMEM((tm + V7X_SUBLANES, conv_cols), F32)],
        compiler_params=pltpu.CompilerParams(dimension_semantics=("arbitrary", "arbitrary"),
                                             vmem_limit_bytes=_vmem_limit(est + (8 << 20))),
        name="gdn_in",
    )(h, norm_w, w_qkvz, w_b, w_a, conv_w, alog, dtb)
    return outs


def _hi_lo(x):
    hi = x.astype(BF16)
    return hi, (x - hi.astype(F32)).astype(BF16)


def _unit_lower_inverse(a_list, eye):
    n = range(len(a_list))
    C = a_list[0].shape[0]
    p = [eye - a for a in a_list]
    x = [a.astype(BF16) for a in a_list]
    for _ in range(C.bit_length() - 2):
        x = [_dot(x[i], x[i]).astype(BF16) for i in n]
        p = [p[i] + _dot(p[i].astype(BF16), x[i]) for i in n]
    a_hi = [a.astype(BF16) for a in a_list]
    p_hl = [_hi_lo(t) for t in p]
    resid = [eye - p[i] - (_dot(a_hi[i], p_hl[i][0]) + _dot(a_hi[i], p_hl[i][1])) for i in n]
    return [p[i] + _dot(p_hl[i][0], resid[i].astype(BF16)) for i in n]


def _gdn_core_kernel(q_ref, k_ref, kb_ref, vb_ref, kbd_ref, qd_ref, kd_ref, zg_ref, l_ref, cd_ref, nw_ref,
                     o_ref, s_ref):
    C = GDN_CHUNK
    n_batch = q_ref.shape[0]

    @pl.when(pl.program_id(0) == 0)
    def _():
        s_ref[...] = jnp.zeros_like(s_ref)

    row = lax.broadcasted_iota(jnp.int32, (C, C), 0)
    col = lax.broadcasted_iota(jnp.int32, (C, C), 1)
    strict = col < row
    eye = (col == row).astype(F32)
    chains = [(b, slice(h * GDN_DK, (h + 1) * GDN_DK)) for b in range(n_batch) for h in range(GDN_HEADS)]
    n = range(len(chains))
    kq = [_dot_nt(jnp.concatenate([kb_ref[b, :, hs], q_ref[b, :, hs]], axis=0), k_ref[b, :, hs])
          for b, hs in chains]
    a = [jnp.where(strict, kq[i][:C] * l_ref[b, :, hs], 0.0) for i, (b, hs) in enumerate(chains)]
    aqk = [(kq[i][C:] * l_ref[b, :, hs]).astype(BF16) for i, (b, hs) in enumerate(chains)]
    t = [_hi_lo(t_i) for t_i in _unit_lower_inverse(a, eye)]
    uw = []
    for i, (b, hs) in enumerate(chains):
        rhs = jnp.concatenate([vb_ref[b, :, hs], kbd_ref[b, :, hs]], axis=1)
        uw.append(_dot(t[i][0], rhs) + _dot(t[i][1], rhs))
    state = [s_ref[i] for i in n]
    ws_qs = [_dot(jnp.concatenate([uw[i][:, GDN_DV:].astype(BF16), qd_ref[b, :, hs]], axis=0),
                  state[i].astype(BF16)) for i, (b, hs) in enumerate(chains)]
    vnb = [(uw[i][:, :GDN_DV] - ws_qs[i][:C]).astype(BF16) for i in n]
    o = [ws_qs[i][C:] + _dot(aqk[i], vnb[i]) for i in n]
    new_state = [state[i] * cd_ref[b, :, hs] + _dot_tn(kd_ref[b, :, hs], vnb[i])
                 for i, (b, hs) in enumerate(chains)]
    for i, (b, hs) in enumerate(chains):
        s_ref[i] = new_state[i]
        on = o[i] * lax.rsqrt(jnp.mean(o[i] * o[i], axis=-1, keepdims=True) + NORM_EPS) * nw_ref[...]
        o_ref[b, :, hs] = (on * zg_ref[b, :, hs].astype(F32)).astype(BF16)


def _gdn_core(ops, norm_w, j):
    q = ops[0]
    B, S, hw = q.shape
    C = GDN_CHUNK
    blk = pl.BlockSpec((B, C, hw), lambda i: (0, i, 0))
    est = 2 * (9 * B * C * hw * 2 + B * C * hw * 4) + 40 * B * GDN_HEADS * C * C * 4
    return pl.pallas_call(
        _gdn_core_kernel,
        grid=(S // C,),
        in_specs=[blk] * 9 + [pl.BlockSpec((B, None, 1, hw), lambda i: (0, i, 0, 0)),
                              _const_spec((None, 1, GDN_DV), lambda i: (j, 0, 0))],
        out_specs=blk,
        out_shape=jax.ShapeDtypeStruct((B, S, hw), BF16),
        scratch_shapes=[pltpu.VMEM((B * GDN_HEADS, GDN_DK, GDN_DV), F32)],
        compiler_params=pltpu.CompilerParams(dimension_semantics=("arbitrary",),
                                             vmem_limit_bytes=_vmem_limit(est)),
        name="gdn_core",
    )(*ops, norm_w)


def _pad_lanes(t):
    return jnp.pad(t, [(0, 0)] * (t.ndim - 1) + [(0, V7X_LANES - t.shape[-1])])


def kernel(x, positions, ffn1_norm, ffn1_w_gate_up, ffn1_w_down, mix_norm, ffn2_norm, ffn2_w_gate_up,
           ffn2_w_down, attn_w_in, attn_b_in, attn_sinks, attn_w_out, attn_b_out, gdn_w_in, gdn_conv_w,
           gdn_A_log, gdn_dt_bias, gdn_norm_w, gdn_w_out, final_norm):
    B, S, D = x.shape
    T = B * S
    depth = ffn1_norm.shape[0]
    assert S % ATTN_BLOCK == 0 and S % GDN_CHUNK == 0 and T % min(FFN_TM, T) == 0

    w1gu, w1d = ffn1_w_gate_up.astype(BF16), ffn1_w_down.astype(BF16)
    w2gu, w2d = ffn2_w_gate_up.astype(BF16), ffn2_w_down.astype(BF16)
    n1, nm, n2 = (t[:, None, :] for t in (ffn1_norm, mix_norm, ffn2_norm))
    a_w_in, a_w_out = attn_w_in.astype(BF16), attn_w_out.astype(BF16)
    a_b_in, a_b_out = attn_b_in[:, None, :], attn_b_out[:, None, :]
    conv_cols = gdn_conv_w.shape[-1]
    z_cols = GDN_HEADS * GDN_DV
    g_w_qkvz = gdn_w_in[:, :, :conv_cols + z_cols].astype(BF16)
    g_w_b = _pad_lanes(gdn_w_in[:, :, conv_cols + z_cols:conv_cols + z_cols + GDN_HEADS]).astype(BF16)
    g_w_a = _pad_lanes(gdn_w_in[:, :, conv_cols + z_cols + GDN_HEADS:]).astype(BF16)
    g_alog = _pad_lanes(gdn_A_log)[:, None, :]
    g_dtb = _pad_lanes(gdn_dt_bias)[:, None, :]
    g_nw = gdn_norm_w[:, None, :]
    g_w_out = gdn_w_out.astype(BF16)
    lane_d = jnp.arange(V7X_LANES) % ATTN_HEAD_DIM
    inv_freq = ROPE_THETA ** (-jnp.arange(0, ROPE_DIM, 2, dtype=F32) / ROPE_DIM)
    invf = jnp.where(lane_d < ROPE_DIM, inv_freq[lane_d % (ROPE_DIM // 2)], 0.0)[None, :].astype(F32)
    pos = positions.reshape(T, 1)

    h = x.reshape(T, D)
    proj = None
    for layer in range(depth):
        h = _ffn(h, n1, w1gu, w1d, layer, proj=proj)
        j = layer // 2
        if layer % 2 == 0:
            q, krep, vrep = _attn_qkv(h, pos, invf, nm, a_w_in, a_b_in, layer, j)
            o = _attn_core(q.reshape(B, S, -1), krep.reshape(B, S, -1), vrep.reshape(B, S, -1), attn_sinks[j])
            proj = (o.reshape(T, -1), a_w_out, j, a_b_out)
        else:
            ops = _gdn_in(h.reshape(B, S, D), nm, g_w_qkvz, g_w_b, g_w_a, gdn_conv_w, g_alog, g_dtb, layer, j)
            o = _gdn_core(ops, g_nw, j)
            proj = (o.reshape(T, -1), g_w_out, j, None)
        last = layer == depth - 1
        h = _ffn(h, n2, w2gu, w2d, layer, proj=proj, final_w=final_norm[None, :] if last else None)
        proj = None
    return h.reshape(B, S, D)
```

```python
import functools

import jax
import jax.numpy as jnp
from jax import lax
from jax.experimental import pallas as pl
from jax.experimental.pallas import tpu as pltpu

F32 = jnp.float32
BF16 = jnp.bfloat16

NORM_EPS = 1e-6

ATTN_Q_HEADS = 16
ATTN_KV_HEADS = 4
ATTN_HEAD_DIM = 64
ATTN_GROUP = ATTN_Q_HEADS // ATTN_KV_HEADS
ATTN_BLOCK = 128
ROPE_DIM = ATTN_HEAD_DIM // 4
ROPE_THETA = 500000.0
GDN_HEADS = 8
GDN_DK = 128
GDN_DV = 128
GDN_CONV = 4
GDN_CHUNK = 128

V7X_LANES = 128
V7X_SUBLANES = 8
V7X_VMEM_BYTES = 64 * 1024 * 1024

FFN_TM = 1024
FFN_SUB = 256
FFN_FF_CHUNK = 256
QKV_TM = 512
ATTN_TQ = 256
GDN_IN_TM = 256


def _vmem_limit(estimate_bytes):
    return int(min(V7X_VMEM_BYTES - (4 << 20), max(estimate_bytes, 16 << 20)))


def _dot(a, b):
    return jnp.dot(a, b, preferred_element_type=F32)


def _dot_nt(a, b):
    return lax.dot_general(a, b, (((1,), (1,)), ((), ())), preferred_element_type=F32)


def _dot_tn(a, b):
    return lax.dot_general(a, b, (((0,), (0,)), ((), ())), preferred_element_type=F32)


def _rms(x, w):
    ms = jnp.mean(x * x, axis=-1, keepdims=True)
    return x * lax.rsqrt(ms + NORM_EPS) * w


def _silu(x):
    return (0.5 * x) * (1.0 + jnp.tanh(0.5 * x))


def _const_spec(shape, index_map):
    return pl.BlockSpec(shape, index_map, pipeline_mode=pl.Buffered(1))


def _ffn_kernel(*refs, d_ff, ff_chunk, sub, has_proj, has_bias, has_final, has_next_norm):
    refs = list(refs)
    h_ref = refs.pop(0)
    if has_proj:
        o_ref = refs.pop(0)
        wo_ref = refs.pop(0)
    if has_bias:
        bo_ref = refs.pop(0)
    nw_ref = refs.pop(0)
    wgu_ref = refs.pop(0)
    wd_ref = refs.pop(0)
    if has_final:
        fw_ref = refs.pop(0)
    if has_next_norm:
        mw_ref = refs.pop(0)
        out_ref, xn_ref, act_ref = refs
    else:
        out_ref, act_ref = refs

    subs = [slice(r, r + sub) for r in range(0, h_ref.shape[0], sub)]
    xs, xns = [], []
    for rs in subs:
        x = h_ref[rs, :]
        if has_proj:
            x = x + _dot(o_ref[rs, :], wo_ref[...])
        if has_bias:
            x = x + bo_ref[...]
        xs.append(x)
        xns.append(_rms(x, nw_ref[...]).astype(BF16))
    for c in range(d_ff // ff_chunk):
        lo = c * ff_chunk
        for rs, xn in zip(subs, xns):
            g = _dot(xn, wgu_ref[:, lo:lo + ff_chunk])
            u = _dot(xn, wgu_ref[:, d_ff + lo:d_ff + lo + ff_chunk])
            act_ref[rs, lo:lo + ff_chunk] = (_silu(g) * u).astype(BF16)
    for rs, x in zip(subs, xs):
        y = x + 0.5 * _dot(act_ref[rs, :], wd_ref[...])
        if has_final:
            y = _rms(y, fw_ref[...])
        out_ref[rs, :] = y
        if has_next_norm:
            xn_ref[rs, :] = _rms(y, mw_ref[...]).astype(BF16)


def _ffn(h, norm_w, wgu, wd, layer, proj=None, final_w=None, next_norm_w=None):
    T, D = h.shape
    d_ff = wd.shape[1]
    tm = min(FFN_TM, T)
    ff_chunk = FFN_FF_CHUNK if d_ff % FFN_FF_CHUNK == 0 else d_ff
    args = [h]
    specs = [pl.BlockSpec((tm, D), lambda i: (i, 0))]
    est = 4 * tm * D * 4 + wgu.shape[1] * wgu.shape[2] * 2 + d_ff * D * 2 + tm * d_ff * 2 + 6 * tm * ff_chunk * 4
    has_proj = proj is not None
    has_bias = has_proj and proj[3] is not None
    if has_proj:
        o, w_out, j, bias = proj
        P = o.shape[1]
        args += [o, w_out]
        specs += [pl.BlockSpec((tm, P), lambda i: (i, 0)),
                  _const_spec((None, P, D), lambda i: (j, 0, 0))]
        est += 2 * tm * P * 2 + P * D * 2
        if has_bias:
            args.append(bias)
            specs.append(_const_spec((None, 1, D), lambda i: (j, 0, 0)))
    args += [norm_w, wgu, wd]
    specs += [_const_spec((None, 1, D), lambda i: (layer, 0, 0)),
              _const_spec((None, D, 2 * d_ff), lambda i: (layer, 0, 0)),
              _const_spec((None, d_ff, D), lambda i: (layer, 0, 0))]
    if final_w is not None:
        args.append(final_w)
        specs.append(_const_spec((1, D), lambda i: (0, 0)))
    tile = pl.BlockSpec((tm, D), lambda i: (i, 0))
    out_specs, out_shape = tile, jax.ShapeDtypeStruct((T, D), F32)
    if next_norm_w is not None:
        args.append(next_norm_w)
        specs.append(_const_spec((None, 1, D), lambda i: (layer, 0, 0)))
        out_specs, out_shape = [tile, tile], [out_shape, jax.ShapeDtypeStruct((T, D), BF16)]
        est += 2 * tm * D * 2
    kern = functools.partial(_ffn_kernel, d_ff=d_ff, ff_chunk=ff_chunk, sub=min(FFN_SUB, tm),
                             has_proj=has_proj, has_bias=has_bias,
                             has_final=final_w is not None, has_next_norm=next_norm_w is not None)
    return pl.pallas_call(
        kern,
        grid=(T // tm,),
        in_specs=specs,
        out_specs=out_specs,
        out_shape=out_shape,
        scratch_shapes=[pltpu.VMEM((tm, d_ff), BF16)],
        compiler_params=pltpu.CompilerParams(dimension_semantics=("parallel",),
                                             vmem_limit_bytes=_vmem_limit(est + (8 << 20))),
        name="ffn",
    )(*args)


def _attn_qkv_kernel(xn_ref, pos_ref, invf_ref, w_ref, b_ref, q_ref, k_ref, v_ref, *, q_w, kv_w):
    qkv = _dot(xn_ref[...], w_ref[...]) + b_ref[...]
    ang = pos_ref[...].astype(F32) * invf_ref[...]
    cos = jnp.cos(ang)
    sin = jnp.sin(ang)
    lane = lax.broadcasted_iota(jnp.int32, (1, V7X_LANES), 1)
    d = lane % ATTN_HEAD_DIM
    first_half = d < (ROPE_DIM // 2)
    sin_signed = jnp.where(first_half, -sin, sin)
    half = ROPE_DIM // 2

    def rope(t):
        up = pltpu.roll(t, V7X_LANES - half, axis=1)
        dn = pltpu.roll(t, half, axis=1)
        return t * cos + jnp.where(first_half, up, dn) * sin_signed

    scale = ATTN_HEAD_DIM ** -0.5
    for s in range(q_w // V7X_LANES):
        sl = slice(s * V7X_LANES, (s + 1) * V7X_LANES)
        q_ref[:, sl] = (rope(qkv[:, sl]) * scale).astype(BF16)
    low = lane < ATTN_HEAD_DIM
    for s in range(kv_w // V7X_LANES):
        ks = rope(qkv[:, q_w + s * V7X_LANES:q_w + (s + 1) * V7X_LANES])
        vs = qkv[:, q_w + kv_w + s * V7X_LANES:q_w + kv_w + (s + 1) * V7X_LANES]
        for t, ref in ((ks, k_ref), (vs, v_ref)):
            sw = pltpu.roll(t, ATTN_HEAD_DIM, axis=1)
            ref[:, (2 * s) * V7X_LANES:(2 * s + 1) * V7X_LANES] = jnp.where(low, t, sw).astype(BF16)
            ref[:, (2 * s + 1) * V7X_LANES:(2 * s + 2) * V7X_LANES] = jnp.where(low, sw, t).astype(BF16)


def _attn_qkv(xn, pos, invf, w_in, b_in, j):
    T, D = xn.shape
    q_w = ATTN_Q_HEADS * ATTN_HEAD_DIM
    kv_w = ATTN_KV_HEADS * ATTN_HEAD_DIM
    n_in = q_w + 2 * kv_w
    rep_w = ATTN_KV_HEADS * V7X_LANES
    tm = min(QKV_TM, T)
    est = 2 * tm * D * 2 + D * n_in * 2 + 3 * tm * n_in * 4 + 2 * tm * (q_w + 2 * rep_w) * 2
    kern = functools.partial(_attn_qkv_kernel, q_w=q_w, kv_w=kv_w)
    return pl.pallas_call(
        kern,
        grid=(T // tm,),
        in_specs=[pl.BlockSpec((tm, D), lambda i: (i, 0)),
                  pl.BlockSpec((tm, 1), lambda i: (i, 0)),
                  _const_spec((1, V7X_LANES), lambda i: (0, 0)),
                  _const_spec((None, D, n_in), lambda i: (j, 0, 0)),
                  _const_spec((None, 1, n_in), lambda i: (j, 0, 0))],
        out_specs=[pl.BlockSpec((tm, q_w), lambda i: (i, 0)),
                   pl.BlockSpec((tm, rep_w), lambda i: (i, 0)),
                   pl.BlockSpec((tm, rep_w), lambda i: (i, 0))],
        out_shape=[jax.ShapeDtypeStruct((T, q_w), BF16),
                   jax.ShapeDtypeStruct((T, rep_w), BF16),
                   jax.ShapeDtypeStruct((T, rep_w), BF16)],
        compiler_params=pltpu.CompilerParams(dimension_semantics=("parallel",),
                                             vmem_limit_bytes=_vmem_limit(est + (8 << 20))),
        name="attn_qkv",
    )(xn, pos, invf, w_in, b_in)


def _attn_core_kernel(sink_ref, q_ref, kc_ref, kp_ref, vc_ref, vp_ref, o_ref, *, tq):
    blk = ATTN_BLOCK
    first_tile = pl.program_id(1) == 0
    row = lax.broadcasted_iota(jnp.int32, (blk, blk), 0)
    col = lax.broadcasted_iota(jnp.int32, (blk, blk), 1)
    tril = col <= row
    lane = lax.broadcasted_iota(jnp.int32, (1, V7X_LANES), 1)
    half_mask = (lane < ATTN_HEAD_DIM, lane >= ATTN_HEAD_DIM)
    no_prev = jnp.where(first_tile, -jnp.inf, 0.0).astype(F32)
    n_blocks = tq // blk
    rows = [slice(j * blk, (j + 1) * blk) for j in range(n_blocks)]
    qsl = [slice(p * V7X_LANES, (p + 1) * V7X_LANES) for p in range(ATTN_Q_HEADS // 2)]
    km, vm = [], []
    for j in range(n_blocks):
        for h in range(ATTN_KV_HEADS):
            hs = slice(h * V7X_LANES, (h + 1) * V7X_LANES)
            kp = kp_ref[:, hs] if j == 0 else kc_ref[rows[j - 1], hs]
            vp = vp_ref[:, hs] if j == 0 else vc_ref[rows[j - 1], hs]
            kband = jnp.concatenate([kp, kc_ref[rows[j], hs]], axis=0)
            vband = jnp.concatenate([vp, vc_ref[rows[j], hs]], axis=0)
            km.append([jnp.where(m, kband, jnp.zeros_like(kband)) for m in half_mask])
            vm.append([jnp.where(m, vband, jnp.zeros_like(vband)) for m in half_mask])
    chains = [(j, n) for j in range(n_blocks) for n in range(ATTN_Q_HEADS)]
    s = [_dot_nt(q_ref[rows[j], qsl[n // 2]], km[j * ATTN_KV_HEADS + n // ATTN_GROUP][n % 2])
         for j, n in chains]
    pband = []
    for i, (j, n) in enumerate(chains):
        s_prev = s[i][:, :blk]
        if j == 0:
            s_prev = s_prev + no_prev
        sc = jnp.where(tril, s[i][:, blk:], s_prev)
        sink = sink_ref[n]
        m = jnp.maximum(jnp.max(sc, axis=-1, keepdims=True), sink)
        pr = jnp.exp(sc - m)
        denom = jnp.sum(pr, axis=-1, keepdims=True) + jnp.exp(sink - m)
        pn = pr * (1.0 / denom)
        zero = jnp.zeros_like(pn)
        pband.append(jnp.concatenate([jnp.where(tril, zero, pn), jnp.where(tril, pn, zero)],
                                     axis=1).astype(BF16))
    pv = [_dot(pband[i], vm[j * ATTN_KV_HEADS + n // ATTN_GROUP][n % 2]) for i, (j, n) in enumerate(chains)]
    for i in range(0, len(chains), 2):
        j, n = chains[i]
        o_ref[rows[j], qsl[n // 2]] = (pv[i] + pv[i + 1]).astype(BF16)


def _attn_core(q, krep, vrep, sinks):
    B, S, q_w = q.shape
    rep_w = krep.shape[-1]
    tq = min(ATTN_TQ, S)
    nb = tq // ATTN_BLOCK
    kern = functools.partial(_attn_core_kernel, tq=tq)
    cur = lambda b, i: (b, i, 0)
    prev = lambda b, i: (b, jnp.maximum(i * nb - 1, 0), 0)
    return pl.pallas_call(
        kern,
        grid=(B, S // tq),
        in_specs=[pl.BlockSpec(memory_space=pltpu.SMEM),
                  pl.BlockSpec((None, tq, q_w), cur),
                  pl.BlockSpec((None, tq, rep_w), cur),
                  pl.BlockSpec((None, ATTN_BLOCK, rep_w), prev),
                  pl.BlockSpec((None, tq, rep_w), cur),
                  pl.BlockSpec((None, ATTN_BLOCK, rep_w), prev)],
        out_specs=pl.BlockSpec((None, tq, q_w), cur),
        out_shape=jax.ShapeDtypeStruct((B, S, q_w), BF16),
        compiler_params=pltpu.CompilerParams(dimension_semantics=("parallel", "parallel")),
        name="attn_core",
    )(sinks, q, krep, krep, vrep, vrep)


def _split3(x):
    a = x.astype(BF16)
    r = x - a.astype(F32)
    b = r.astype(BF16)
    c = (r - b.astype(F32)).astype(BF16)
    return a, b, c


def _gdn_in_kernel(xn_ref, w_ref, wb_ref, wa_ref, cw_ref, alog_ref, dtb_ref,
                   q_ref, k_ref, kb_ref, vb_ref, kbd_ref, qd_ref, kd_ref, z_ref, l_ref, cd_ref,
                   buf_ref, *, tm):
    C = GDN_CHUNK
    qk_w = GDN_HEADS * GDN_DK
    conv_w = 2 * qk_w + GDN_HEADS * GDN_DV
    halo = V7X_SUBLANES

    @pl.when(pl.program_id(1) == 0)
    def _():
        buf_ref[0:halo, :] = jnp.zeros((halo, conv_w), F32)

    xn = xn_ref[...]
    buf_ref[halo:halo + tm, :] = _dot(xn, w_ref[:, :conv_w])
    z_ref[...] = _dot(xn, w_ref[:, conv_w:]).astype(BF16)
    beta = jax.nn.sigmoid(_dot(xn, wb_ref[...]))
    a = _dot(xn, wa_ref[...])
    g = -jnp.exp(alog_ref[...]) * jax.nn.softplus(a + dtb_ref[...])

    row = lax.broadcasted_iota(jnp.int32, (C, C), 0)
    col = lax.broadcasted_iota(jnp.int32, (C, C), 1)
    causal = col <= row
    tril = causal.astype(BF16)

    def conv_silu(lo):
        xf = buf_ref[:, lo:lo + V7X_LANES]
        acc = cw_ref[GDN_CONV - 1:GDN_CONV, lo:lo + V7X_LANES] * xf
        for s in range(1, GDN_CONV):
            acc = acc + cw_ref[GDN_CONV - 1 - s:GDN_CONV - s, lo:lo + V7X_LANES] * pltpu.roll(xf, s, axis=0)
        return _silu(acc[halo:halo + tm])

    def l2n(t):
        return t * lax.rsqrt(jnp.sum(t * t, axis=-1, keepdims=True) + NORM_EPS)

    d_cols, d_rows, e_cols, kd_cols = [], [], [], []
    for c in range(tm // C):
        g1, g2, g3 = _split3(g[c * C:(c + 1) * C, :])
        dc = _dot(tril, g1) + _dot(tril, g2) + _dot(tril, g3)
        d_cols.append(dc)
        d_rows.append(dc.T)
        e_cols.append(jnp.exp(dc))
        kd_cols.append(jnp.exp(dc[C - 1:C, :] - dc))

    for h in range(GDN_HEADS):
        hs = slice(h * GDN_DK, (h + 1) * GDN_DK)
        qh = l2n(conv_silu(h * GDN_DK)) * (GDN_DK ** -0.5)
        kh = l2n(conv_silu(qk_w + h * GDN_DK))
        vh = conv_silu(2 * qk_w + h * GDN_DV)
        for c in range(tm // C):
            rs = slice(c * C, (c + 1) * C)
            dB = jnp.broadcast_to(d_cols[c][:, h:h + 1], (C, C))
            bB = jnp.broadcast_to(beta[rs, h:h + 1], (C, C))
            eB = jnp.broadcast_to(e_cols[c][:, h:h + 1], (C, C))
            kdB = jnp.broadcast_to(kd_cols[c][:, h:h + 1], (C, C))
            dR = jnp.broadcast_to(d_rows[c][h:h + 1, :], (C, C))
            q_c, k_c, v_c = qh[rs], kh[rs], vh[rs]
            kb = k_c * bB
            q_ref[rs, hs] = q_c.astype(BF16)
            k_ref[rs, hs] = k_c.astype(BF16)
            kb_ref[rs, hs] = kb.astype(BF16)
            vb_ref[rs, hs] = (v_c * bB).astype(BF16)
            kbd_ref[rs, hs] = (kb * eB).astype(BF16)
            qd_ref[rs, hs] = (q_c * eB).astype(BF16)
            kd_ref[rs, hs] = (k_c * kdB).astype(BF16)
            l_ref[rs, hs] = jnp.exp(jnp.where(causal, dB - dR, -jnp.inf))
            cd_ref[c, :, hs] = eB[C - 1:C, :]

    buf_ref[0:halo, :] = buf_ref[tm:tm + halo, :]


def _gdn_in(xn, w_qkvz, w_b, w_a, conv_w, alog, dtb, j):
    B, S, D = xn.shape
    hw = GDN_HEADS * GDN_DK
    conv_cols = conv_w.shape[-1]
    tm = min(GDN_IN_TM, S)
    n_chunks = S // GDN_CHUNK
    tile = lambda b, i: (b, i, 0)
    big = pl.BlockSpec((None, tm, hw), tile)
    est = (2 * tm * D * 2 + D * w_qkvz.shape[-1] * 2 + 2 * D * V7X_LANES * 2 + (tm + 8) * conv_cols * 4
           + 2 * 8 * tm * hw * 2 + 2 * tm * hw * 4 + 3 * tm * hw * 4)
    kern = functools.partial(_gdn_in_kernel, tm=tm)
    outs = pl.pallas_call(
        kern,
        grid=(B, S // tm),
        in_specs=[pl.BlockSpec((None, tm, D), tile),
                  _const_spec((None, D, w_qkvz.shape[-1]), lambda b, i: (j, 0, 0)),
                  _const_spec((None, D, V7X_LANES), lambda b, i: (j, 0, 0)),
                  _const_spec((None, D, V7X_LANES), lambda b, i: (j, 0, 0)),
                  _const_spec((None, GDN_CONV, conv_cols), lambda b, i: (j, 0, 0)),
                  _const_spec((None, 1, V7X_LANES), lambda b, i: (j, 0, 0)),
                  _const_spec((None, 1, V7X_LANES), lambda b, i: (j, 0, 0))],
        out_specs=[big] * 8 + [pl.BlockSpec((None, tm, hw), tile),
                               pl.BlockSpec((None, tm // GDN_CHUNK, 1, hw), lambda b, i: (b, i, 0, 0))],
        out_shape=[jax.ShapeDtypeStruct((B, S, hw), BF16)] * 8
                  + [jax.ShapeDtypeStruct((B, S, hw), F32),
                     jax.ShapeDtypeStruct((B, n_chunks, 1, hw), F32)],
        scratch_shapes=[pltpu.VMEM((tm + V7X_SUBLANES, conv_cols), F32)],
        compiler_params=pltpu.CompilerParams(dimension_semantics=("arbitrary", "arbitrary"),
                                             vmem_limit_bytes=_vmem_limit(est + (8 << 20))),
        name="gdn_in",
    )(xn, w_qkvz, w_b, w_a, conv_w, alog, dtb)
    return outs


def _hi_lo(x):
    hi = x.astype(BF16)
    return hi, (x - hi.astype(F32)).astype(BF16)


def _dot2(a1, b1, a2, b2):
    return _dot(jnp.concatenate([a1, a2], axis=1), jnp.concatenate([b1, b2], axis=0))


def _unit_lower_inverse(a_list, eye):
    n = range(len(a_list))
    C = a_list[0].shape[0]
    p = [eye - a for a in a_list]
    x = [a.astype(BF16) for a in a_list]
    for _ in range(C.bit_length() - 2):
        x = [_dot(x[i], x[i]).astype(BF16) for i in n]
        p = [p[i] + _dot(p[i].astype(BF16), x[i]) for i in n]
    a_hi = [a.astype(BF16) for a in a_list]
    p_hl = [_hi_lo(t) for t in p]
    resid = [eye - p[i] - _dot2(a_hi[i], p_hl[i][0], a_hi[i], p_hl[i][1]) for i in n]
    return [p[i] + _dot(p_hl[i][0], resid[i].astype(BF16)) for i in n]


def _gdn_core_kernel(q_ref, k_ref, kb_ref, vb_ref, kbd_ref, qd_ref, kd_ref, z_ref, l_ref, cd_ref, nw_ref,
                     o_ref, s_ref):
    C = GDN_CHUNK
    n_batch = q_ref.shape[0]

    @pl.when(pl.program_id(0) == 0)
    def _():
        s_ref[...] = jnp.zeros_like(s_ref)

    row = lax.broadcasted_iota(jnp.int32, (C, C), 0)
    col = lax.broadcasted_iota(jnp.int32, (C, C), 1)
    strict = col < row
    eye = (col == row).astype(F32)
    chains = [(b, slice(h * GDN_DK, (h + 1) * GDN_DK)) for b in range(n_batch) for h in range(GDN_HEADS)]
    n = range(len(chains))
    kq = [_dot_nt(jnp.concatenate([kb_ref[b, :, hs], q_ref[b, :, hs]], axis=0), k_ref[b, :, hs])
          for b, hs in chains]
    a = [jnp.where(strict, kq[i][:C] * l_ref[b, :, hs], 0.0) for i, (b, hs) in enumerate(chains)]
    aqk = [(kq[i][C:] * l_ref[b, :, hs]).astype(BF16) for i, (b, hs) in enumerate(chains)]
    t = [_hi_lo(t_i) for t_i in _unit_lower_inverse(a, eye)]
    uw = []
    for i, (b, hs) in enumerate(chains):
        rhs = jnp.concatenate([vb_ref[b, :, hs], kbd_ref[b, :, hs]], axis=1)
        uw.append(_dot2(t[i][0], rhs, t[i][1], rhs))
    state = [s_ref[i] for i in n]
    sb = [st.astype(BF16) for st in state]
    vnb = [(uw[i][:, :GDN_DV] - _dot(uw[i][:, GDN_DV:].astype(BF16), sb[i])).astype(BF16) for i in n]
    o = [_dot2(qd_ref[b, :, hs], sb[i], aqk[i], vnb[i]) for i, (b, hs) in enumerate(chains)]
    new_state = [state[i] * cd_ref[b, :, hs] + _dot_tn(kd_ref[b, :, hs], vnb[i])
                 for i, (b, hs) in enumerate(chains)]
    for i, (b, hs) in enumerate(chains):
        s_ref[i] = new_state[i]
        on = o[i] * lax.rsqrt(jnp.mean(o[i] * o[i], axis=-1, keepdims=True) + NORM_EPS) * nw_ref[...]
        o_ref[b, :, hs] = (on * _silu(z_ref[b, :, hs].astype(F32))).astype(BF16)


def _gdn_core(ops, norm_w, j):
    q = ops[0]
    B, S, hw = q.shape
    C = GDN_CHUNK
    blk = pl.BlockSpec((B, C, hw), lambda i: (0, i, 0))
    est = 2 * (9 * B * C * hw * 2 + B * C * hw * 4) + 40 * B * GDN_HEADS * C * C * 4
    return pl.pallas_call(
        _gdn_core_kernel,
        grid=(S // C,),
        in_specs=[blk] * 9 + [pl.BlockSpec((B, None, 1, hw), lambda i: (0, i, 0, 0)),
                              _const_spec((None, 1, GDN_DV), lambda i: (j, 0, 0))],
        out_specs=blk,
        out_shape=jax.ShapeDtypeStruct((B, S, hw), BF16),
        scratch_shapes=[pltpu.VMEM((B * GDN_HEADS, GDN_DK, GDN_DV), F32)],
        compiler_params=pltpu.CompilerParams(dimension_semantics=("arbitrary",),
                                             vmem_limit_bytes=_vmem_limit(est)),
        name="gdn_core",
    )(*ops, norm_w)


def _pad_lanes(t):
    return jnp.pad(t, [(0, 0)] * (t.ndim - 1) + [(0, V7X_LANES - t.shape[-1])])


def kernel(x, positions, ffn1_norm, ffn1_w_gate_up, ffn1_w_down, mix_norm, ffn2_norm, ffn2_w_gate_up,
           ffn2_w_down, attn_w_in, attn_b_in, attn_sinks, attn_w_out, attn_b_out, gdn_w_in, gdn_conv_w,
           gdn_A_log, gdn_dt_bias, gdn_norm_w, gdn_w_out, final_norm):
    B, S, D = x.shape
    T = B * S
    depth = ffn1_norm.shape[0]
    assert S % ATTN_BLOCK == 0 and S % GDN_CHUNK == 0 and T % min(FFN_TM, T) == 0

    w1gu, w1d = ffn1_w_gate_up.astype(BF16), ffn1_w_down.astype(BF16)
    w2gu, w2d = ffn2_w_gate_up.astype(BF16), ffn2_w_down.astype(BF16)
    n1, nm, n2 = (t[:, None, :] for t in (ffn1_norm, mix_norm, ffn2_norm))
    a_w_in, a_w_out = attn_w_in.astype(BF16), attn_w_out.astype(BF16)
    a_b_in, a_b_out = attn_b_in[:, None, :], attn_b_out[:, None, :]
    conv_cols = gdn_conv_w.shape[-1]
    z_cols = GDN_HEADS * GDN_DV
    g_w_qkvz = gdn_w_in[:, :, :conv_cols + z_cols].astype(BF16)
    g_w_b = _pad_lanes(gdn_w_in[:, :, conv_cols + z_cols:conv_cols + z_cols + GDN_HEADS]).astype(BF16)
    g_w_a = _pad_lanes(gdn_w_in[:, :, conv_cols + z_cols + GDN_HEADS:]).astype(BF16)
    g_alog = _pad_lanes(gdn_A_log)[:, None, :]
    g_dtb = _pad_lanes(gdn_dt_bias)[:, None, :]
    g_nw = gdn_norm_w[:, None, :]
    g_w_out = gdn_w_out.astype(BF16)
    lane_d = jnp.arange(V7X_LANES) % ATTN_HEAD_DIM
    inv_freq = ROPE_THETA ** (-jnp.arange(0, ROPE_DIM, 2, dtype=F32) / ROPE_DIM)
    invf = jnp.where(lane_d < ROPE_DIM, inv_freq[lane_d % (ROPE_DIM // 2)], 0.0)[None, :].astype(F32)
    pos = positions.reshape(T, 1)

    h = x.reshape(T, D)
    for layer in range(depth):
        h, xn = _ffn(h, n1, w1gu, w1d, layer, next_norm_w=nm)
        j = layer // 2
        if layer % 2 == 0:
            q, krep, vrep = _attn_qkv(xn, pos, invf, a_w_in, a_b_in, j)
            o = _attn_core(q.reshape(B, S, -1), krep.reshape(B, S, -1), vrep.reshape(B, S, -1), attn_sinks[j])
            proj = (o.reshape(T, -1), a_w_out, j, a_b_out)
        else:
            ops = _gdn_in(xn.reshape(B, S, D), g_w_qkvz, g_w_b, g_w_a, gdn_conv_w, g_alog, g_dtb, j)
            o = _gdn_core(ops, g_nw, j)
            proj = (o.reshape(T, -1), g_w_out, j, None)
        last = layer == depth - 1
        h = _ffn(h, n2, w2gu, w2d, layer, proj=proj, final_w=final_norm[None, :] if last else None)
    return h.reshape(B, S, D)
```

```python
import functools

import jax
import jax.numpy as jnp
from jax import lax
from jax.experimental import pallas as pl
from jax.experimental.pallas import tpu as pltpu

F32 = jnp.float32
BF16 = jnp.bfloat16

NORM_EPS = 1e-6

ATTN_Q_HEADS = 16
ATTN_KV_HEADS = 4
ATTN_HEAD_DIM = 64
ATTN_GROUP = ATTN_Q_HEADS // ATTN_KV_HEADS
ATTN_BLOCK = 128
ROPE_DIM = ATTN_HEAD_DIM // 4
ROPE_THETA = 500000.0
GDN_HEADS = 8
GDN_DK = 128
GDN_DV = 128
GDN_CONV = 4
GDN_CHUNK = 128

V7X_LANES = 128
V7X_SUBLANES = 8
V7X_VMEM_BYTES = 64 * 1024 * 1024

FFN_TM = 1024
FFN_SUB = 256
FFN_FF_CHUNK = 256
QKV_TM = 1024
ATTN_TQ = 256
GDN_IN_TM = 512


def _vmem_limit(estimate_bytes):
    return int(min(V7X_VMEM_BYTES - (4 << 20), max(estimate_bytes, 16 << 20)))


def _dot(a, b):
    return jnp.dot(a, b, preferred_element_type=F32)


def _dot_nt(a, b):
    return lax.dot_general(a, b, (((1,), (1,)), ((), ())), preferred_element_type=F32)


def _dot_tn(a, b):
    return lax.dot_general(a, b, (((0,), (0,)), ((), ())), preferred_element_type=F32)


def _rms(x, w):
    ms = jnp.mean(x * x, axis=-1, keepdims=True)
    return x * lax.rsqrt(ms + NORM_EPS) * w


def _silu(x):
    return (0.5 * x) * (1.0 + jnp.tanh(0.5 * x))


def _const_spec(shape, index_map):
    return pl.BlockSpec(shape, index_map, pipeline_mode=pl.Buffered(1))


def _ffn_kernel(*refs, d_ff, ff_chunk, sub, has_proj, has_bias, has_final, has_next_norm):
    refs = list(refs)
    h_ref = refs.pop(0)
    if has_proj:
        o_ref = refs.pop(0)
        wo_ref = refs.pop(0)
    if has_bias:
        bo_ref = refs.pop(0)
    nw_ref = refs.pop(0)
    wgu_ref = refs.pop(0)
    wd_ref = refs.pop(0)
    if has_final:
        fw_ref = refs.pop(0)
    if has_next_norm:
        mw_ref = refs.pop(0)
        out_ref, xn_ref, act_ref = refs
    else:
        out_ref, act_ref = refs

    subs = [slice(r, r + sub) for r in range(0, h_ref.shape[0], sub)]
    xs, xns = [], []
    for rs in subs:
        x = h_ref[rs, :]
        if has_proj:
            x = x + _dot(o_ref[rs, :], wo_ref[...])
        if has_bias:
            x = x + bo_ref[...]
        xs.append(x)
        xns.append(_rms(x, nw_ref[...]).astype(BF16))
    for c in range(d_ff // ff_chunk):
        lo = c * ff_chunk
        for rs, xn in zip(subs, xns):
            g = _dot(xn, wgu_ref[:, lo:lo + ff_chunk])
            u = _dot(xn, wgu_ref[:, d_ff + lo:d_ff + lo + ff_chunk])
            act_ref[rs, lo:lo + ff_chunk] = (_silu(g) * u).astype(BF16)
    for rs, x in zip(subs, xs):
        y = x + 0.5 * _dot(act_ref[rs, :], wd_ref[...])
        if has_final:
            y = _rms(y, fw_ref[...])
        out_ref[rs, :] = y
        if has_next_norm:
            xn_ref[rs, :] = _rms(y, mw_ref[...]).astype(BF16)


def _ffn(h, norm_w, wgu, wd, layer, proj=None, final_w=None, next_norm_w=None):
    T, D = h.shape
    d_ff = wd.shape[1]
    tm = min(FFN_TM, T)
    ff_chunk = FFN_FF_CHUNK if d_ff % FFN_FF_CHUNK == 0 else d_ff
    args = [h]
    specs = [pl.BlockSpec((tm, D), lambda i: (i, 0))]
    est = 4 * tm * D * 4 + wgu.shape[1] * wgu.shape[2] * 2 + d_ff * D * 2 + tm * d_ff * 2 + 6 * tm * ff_chunk * 4
    has_proj = proj is not None
    has_bias = has_proj and proj[3] is not None
    if has_proj:
        o, w_out, j, bias = proj
        P = o.shape[1]
        args += [o, w_out]
        specs += [pl.BlockSpec((tm, P), lambda i: (i, 0)),
                  _const_spec((None, P, D), lambda i: (j, 0, 0))]
        est += 2 * tm * P * 2 + P * D * 2
        if has_bias:
            args.append(bias)
            specs.append(_const_spec((None, 1, D), lambda i: (j, 0, 0)))
    args += [norm_w, wgu, wd]
    specs += [_const_spec((None, 1, D), lambda i: (layer, 0, 0)),
              _const_spec((None, D, 2 * d_ff), lambda i: (layer, 0, 0)),
              _const_spec((None, d_ff, D), lambda i: (layer, 0, 0))]
    if final_w is not None:
        args.append(final_w)
        specs.append(_const_spec((1, D), lambda i: (0, 0)))
    tile = pl.BlockSpec((tm, D), lambda i: (i, 0))
    out_specs, out_shape = tile, jax.ShapeDtypeStruct((T, D), F32)
    if next_norm_w is not None:
        args.append(next_norm_w)
        specs.append(_const_spec((None, 1, D), lambda i: (layer, 0, 0)))
        out_specs, out_shape = [tile, tile], [out_shape, jax.ShapeDtypeStruct((T, D), BF16)]
        est += 2 * tm * D * 2
    kern = functools.partial(_ffn_kernel, d_ff=d_ff, ff_chunk=ff_chunk, sub=min(FFN_SUB, tm),
                             has_proj=has_proj, has_bias=has_bias,
                             has_final=final_w is not None, has_next_norm=next_norm_w is not None)
    return pl.pallas_call(
        kern,
        grid=(T // tm,),
        in_specs=specs,
        out_specs=out_specs,
        out_shape=out_shape,
        scratch_shapes=[pltpu.VMEM((tm, d_ff), BF16)],
        compiler_params=pltpu.CompilerParams(dimension_semantics=("parallel",),
                                             vmem_limit_bytes=_vmem_limit(est + (8 << 20))),
        name="ffn",
    )(*args)


def _attn_qkv_kernel(xn_ref, pos_ref, invf_ref, w_ref, b_ref, q_ref, k_ref, v_ref, *, q_w, kv_w):
    qkv = _dot(xn_ref[...], w_ref[...]) + b_ref[...]
    ang = pos_ref[...].astype(F32) * invf_ref[...]
    cos = jnp.cos(ang)
    sin = jnp.sin(ang)
    lane = lax.broadcasted_iota(jnp.int32, (1, V7X_LANES), 1)
    d = lane % ATTN_HEAD_DIM
    first_half = d < (ROPE_DIM // 2)
    sin_signed = jnp.where(first_half, -sin, sin)
    half = ROPE_DIM // 2

    def rope(t):
        up = pltpu.roll(t, V7X_LANES - half, axis=1)
        dn = pltpu.roll(t, half, axis=1)
        return t * cos + jnp.where(first_half, up, dn) * sin_signed

    scale = ATTN_HEAD_DIM ** -0.5
    for s in range(q_w // V7X_LANES):
        sl = slice(s * V7X_LANES, (s + 1) * V7X_LANES)
        q_ref[:, sl] = (rope(qkv[:, sl]) * scale).astype(BF16)
    low = lane < ATTN_HEAD_DIM
    for s in range(kv_w // V7X_LANES):
        ks = rope(qkv[:, q_w + s * V7X_LANES:q_w + (s + 1) * V7X_LANES])
        vs = qkv[:, q_w + kv_w + s * V7X_LANES:q_w + kv_w + (s + 1) * V7X_LANES]
        for t, ref in ((ks, k_ref), (vs, v_ref)):
            sw = pltpu.roll(t, ATTN_HEAD_DIM, axis=1)
            ref[:, (2 * s) * V7X_LANES:(2 * s + 1) * V7X_LANES] = jnp.where(low, t, sw).astype(BF16)
            ref[:, (2 * s + 1) * V7X_LANES:(2 * s + 2) * V7X_LANES] = jnp.where(low, sw, t).astype(BF16)


def _attn_qkv(xn, pos, invf, w_in, b_in, j):
    T, D = xn.shape
    q_w = ATTN_Q_HEADS * ATTN_HEAD_DIM
    kv_w = ATTN_KV_HEADS * ATTN_HEAD_DIM
    n_in = q_w + 2 * kv_w
    rep_w = ATTN_KV_HEADS * V7X_LANES
    tm = min(QKV_TM, T)
    est = 2 * tm * D * 2 + D * n_in * 2 + 3 * tm * n_in * 4 + 2 * tm * (q_w + 2 * rep_w) * 2
    kern = functools.partial(_attn_qkv_kernel, q_w=q_w, kv_w=kv_w)
    return pl.pallas_call(
        kern,
        grid=(T // tm,),
        in_specs=[pl.BlockSpec((tm, D), lambda i: (i, 0)),
                  pl.BlockSpec((tm, 1), lambda i: (i, 0)),
                  _const_spec((1, V7X_LANES), lambda i: (0, 0)),
                  _const_spec((None, D, n_in), lambda i: (j, 0, 0)),
                  _const_spec((None, 1, n_in), lambda i: (j, 0, 0))],
        out_specs=[pl.BlockSpec((tm, q_w), lambda i: (i, 0)),
                   pl.BlockSpec((tm, rep_w), lambda i: (i, 0)),
                   pl.BlockSpec((tm, rep_w), lambda i: (i, 0))],
        out_shape=[jax.ShapeDtypeStruct((T, q_w), BF16),
                   jax.ShapeDtypeStruct((T, rep_w), BF16),
                   jax.ShapeDtypeStruct((T, rep_w), BF16)],
        compiler_params=pltpu.CompilerParams(dimension_semantics=("parallel",),
                                             vmem_limit_bytes=_vmem_limit(est + (8 << 20))),
        name="attn_qkv",
    )(xn, pos, invf, w_in, b_in)


def _attn_core_kernel(sink_ref, q_ref, kc_ref, kp_ref, vc_ref, vp_ref, o_ref, *, tq):
    blk = ATTN_BLOCK
    first_tile = pl.program_id(1) == 0
    row = lax.broadcasted_iota(jnp.int32, (blk, blk), 0)
    col = lax.broadcasted_iota(jnp.int32, (blk, blk), 1)
    tril = col <= row
    lane = lax.broadcasted_iota(jnp.int32, (1, V7X_LANES), 1)
    half_mask = (lane < ATTN_HEAD_DIM, lane >= ATTN_HEAD_DIM)
    no_prev = jnp.where(first_tile, -jnp.inf, 0.0).astype(F32)
    n_blocks = tq // blk
    rows = [slice(j * blk, (j + 1) * blk) for j in range(n_blocks)]
    qsl = [slice(p * V7X_LANES, (p + 1) * V7X_LANES) for p in range(ATTN_Q_HEADS // 2)]
    km, vm = [], []
    for j in range(n_blocks):
        for h in range(ATTN_KV_HEADS):
            hs = slice(h * V7X_LANES, (h + 1) * V7X_LANES)
            kp = kp_ref[:, hs] if j == 0 else kc_ref[rows[j - 1], hs]
            vp = vp_ref[:, hs] if j == 0 else vc_ref[rows[j - 1], hs]
            kband = jnp.concatenate([kp, kc_ref[rows[j], hs]], axis=0)
            vband = jnp.concatenate([vp, vc_ref[rows[j], hs]], axis=0)
            km.append([jnp.where(m, kband, jnp.zeros_like(kband)) for m in half_mask])
            vm.append([jnp.where(m, vband, jnp.zeros_like(vband)) for m in half_mask])
    chains = [(j, n) for j in range(n_blocks) for n in range(ATTN_Q_HEADS)]
    s = [_dot_nt(q_ref[rows[j], qsl[n // 2]], km[j * ATTN_KV_HEADS + n // ATTN_GROUP][n % 2])
         for j, n in chains]
    pband = []
    for i, (j, n) in enumerate(chains):
        s_prev = s[i][:, :blk]
        if j == 0:
            s_prev = s_prev + no_prev
        sc = jnp.where(tril, s[i][:, blk:], s_prev)
        sink = sink_ref[n]
        m = jnp.maximum(jnp.max(sc, axis=-1, keepdims=True), sink)
        pr = jnp.exp(sc - m)
        denom = jnp.sum(pr, axis=-1, keepdims=True) + jnp.exp(sink - m)
        pn = pr * (1.0 / denom)
        zero = jnp.zeros_like(pn)
        pband.append(jnp.concatenate([jnp.where(tril, zero, pn), jnp.where(tril, pn, zero)],
                                     axis=1).astype(BF16))
    pv = [_dot(pband[i], vm[j * ATTN_KV_HEADS + n // ATTN_GROUP][n % 2]) for i, (j, n) in enumerate(chains)]
    for i in range(0, len(chains), 2):
        j, n = chains[i]
        o_ref[rows[j], qsl[n // 2]] = (pv[i] + pv[i + 1]).astype(BF16)


def _attn_core(q, krep, vrep, sinks):
    B, S, q_w = q.shape
    rep_w = krep.shape[-1]
    tq = min(ATTN_TQ, S)
    nb = tq // ATTN_BLOCK
    kern = functools.partial(_attn_core_kernel, tq=tq)
    cur = lambda b, i: (b, i, 0)
    prev = lambda b, i: (b, jnp.maximum(i * nb - 1, 0), 0)
    return pl.pallas_call(
        kern,
        grid=(B, S // tq),
        in_specs=[pl.BlockSpec(memory_space=pltpu.SMEM),
                  pl.BlockSpec((None, tq, q_w), cur),
                  pl.BlockSpec((None, tq, rep_w), cur),
                  pl.BlockSpec((None, ATTN_BLOCK, rep_w), prev),
                  pl.BlockSpec((None, tq, rep_w), cur),
                  pl.BlockSpec((None, ATTN_BLOCK, rep_w), prev)],
        out_specs=pl.BlockSpec((None, tq, q_w), cur),
        out_shape=jax.ShapeDtypeStruct((B, S, q_w), BF16),
        compiler_params=pltpu.CompilerParams(dimension_semantics=("parallel", "parallel")),
        name="attn_core",
    )(sinks, q, krep, krep, vrep, vrep)


def _split3(x):
    a = x.astype(BF16)
    r = x - a.astype(F32)
    b = r.astype(BF16)
    c = (r - b.astype(F32)).astype(BF16)
    return a, b, c


def _gdn_in_kernel(xn_ref, w_ref, wb_ref, wa_ref, cw_ref, alog_ref, dtb_ref,
                   q_ref, k_ref, kb_ref, vb_ref, kbd_ref, qd_ref, kd_ref, z_ref, l_ref, cd_ref,
                   buf_ref, *, tm):
    C = GDN_CHUNK
    qk_w = GDN_HEADS * GDN_DK
    conv_w = 2 * qk_w + GDN_HEADS * GDN_DV
    halo = V7X_SUBLANES

    @pl.when(pl.program_id(1) == 0)
    def _():
        buf_ref[0:halo, :] = jnp.zeros((halo, conv_w), F32)

    xn = xn_ref[...]
    buf_ref[halo:halo + tm, :] = _dot(xn, w_ref[:, :conv_w])
    z_ref[...] = _dot(xn, w_ref[:, conv_w:]).astype(BF16)
    beta = jax.nn.sigmoid(_dot(xn, wb_ref[...]))
    a = _dot(xn, wa_ref[...])
    g = -jnp.exp(alog_ref[...]) * jax.nn.softplus(a + dtb_ref[...])

    row = lax.broadcasted_iota(jnp.int32, (C, C), 0)
    col = lax.broadcasted_iota(jnp.int32, (C, C), 1)
    causal = col <= row
    tril = causal.astype(BF16)

    def conv_silu(lo):
        xf = buf_ref[:, lo:lo + V7X_LANES]
        acc = cw_ref[GDN_CONV - 1:GDN_CONV, lo:lo + V7X_LANES] * xf
        for s in range(1, GDN_CONV):
            acc = acc + cw_ref[GDN_CONV - 1 - s:GDN_CONV - s, lo:lo + V7X_LANES] * pltpu.roll(xf, s, axis=0)
        return _silu(acc[halo:halo + tm])

    def l2n(t):
        return t * lax.rsqrt(jnp.sum(t * t, axis=-1, keepdims=True) + NORM_EPS)

    d_cols, d_rows, e_cols, kd_cols = [], [], [], []
    for c in range(tm // C):
        g1, g2, g3 = _split3(g[c * C:(c + 1) * C, :])
        dc = _dot(tril, g1) + _dot(tril, g2) + _dot(tril, g3)
        d_cols.append(dc)
        d_rows.append(dc.T)
        e_cols.append(jnp.exp(dc))
        kd_cols.append(jnp.exp(dc[C - 1:C, :] - dc))

    heads = range(GDN_HEADS)
    q_act = [conv_silu(h * GDN_DK) for h in heads]
    k_act = [conv_silu(qk_w + h * GDN_DK) for h in heads]
    v_all = [conv_silu(2 * qk_w + h * GDN_DV) for h in heads]
    q_all = [l2n(t) * (GDN_DK ** -0.5) for t in q_act]
    k_all = [l2n(t) for t in k_act]
    for h in heads:
        hs = slice(h * GDN_DK, (h + 1) * GDN_DK)
        qh, kh, vh = q_all[h], k_all[h], v_all[h]
        for c in range(tm // C):
            rs = slice(c * C, (c + 1) * C)
            dB = jnp.broadcast_to(d_cols[c][:, h:h + 1], (C, C))
            bB = jnp.broadcast_to(beta[rs, h:h + 1], (C, C))
            eB = jnp.broadcast_to(e_cols[c][:, h:h + 1], (C, C))
            kdB = jnp.broadcast_to(kd_cols[c][:, h:h + 1], (C, C))
            dR = jnp.broadcast_to(d_rows[c][h:h + 1, :], (C, C))
            q_c, k_c, v_c = qh[rs], kh[rs], vh[rs]
            kb = k_c * bB
            q_ref[rs, hs] = q_c.astype(BF16)
            k_ref[rs, hs] = k_c.astype(BF16)
            kb_ref[rs, hs] = kb.astype(BF16)
            vb_ref[rs, hs] = (v_c * bB).astype(BF16)
            kbd_ref[rs, hs] = (kb * eB).astype(BF16)
            qd_ref[rs, hs] = (q_c * eB).astype(BF16)
            kd_ref[rs, hs] = (k_c * kdB).astype(BF16)
            l_ref[rs, hs] = jnp.exp(jnp.where(causal, dB - dR, -jnp.inf))
            cd_ref[c, :, hs] = eB[C - 1:C, :]

    buf_ref[0:halo, :] = buf_ref[tm:tm + halo, :]


def _gdn_in(xn, w_qkvz, w_b, w_a, conv_w, alog, dtb, j):
    B, S, D = xn.shape
    hw = GDN_HEADS * GDN_DK
    conv_cols = conv_w.shape[-1]
    tm = min(GDN_IN_TM, S)
    n_chunks = S // GDN_CHUNK
    tile = lambda b, i: (b, i, 0)
    big = pl.BlockSpec((None, tm, hw), tile)
    est = (2 * tm * D * 2 + D * w_qkvz.shape[-1] * 2 + 2 * D * V7X_LANES * 2 + (tm + 8) * conv_cols * 4
           + 2 * 8 * tm * hw * 2 + 2 * tm * hw * 4 + 3 * tm * hw * 4)
    kern = functools.partial(_gdn_in_kernel, tm=tm)
    outs = pl.pallas_call(
        kern,
        grid=(B, S // tm),
        in_specs=[pl.BlockSpec((None, tm, D), tile),
                  _const_spec((None, D, w_qkvz.shape[-1]), lambda b, i: (j, 0, 0)),
                  _const_spec((None, D, V7X_LANES), lambda b, i: (j, 0, 0)),
                  _const_spec((None, D, V7X_LANES), lambda b, i: (j, 0, 0)),
                  _const_spec((None, GDN_CONV, conv_cols), lambda b, i: (j, 0, 0)),
                  _const_spec((None, 1, V7X_LANES), lambda b, i: (j, 0, 0)),
                  _const_spec((None, 1, V7X_LANES), lambda b, i: (j, 0, 0))],
        out_specs=[big] * 8 + [pl.BlockSpec((None, tm, hw), tile),
                               pl.BlockSpec((None, tm // GDN_CHUNK, 1, hw), lambda b, i: (b, i, 0, 0))],
        out_shape=[jax.ShapeDtypeStruct((B, S, hw), BF16)] * 8
                  + [jax.ShapeDtypeStruct((B, S, hw), F32),
                     jax.ShapeDtypeStruct((B, n_chunks, 1, hw), F32)],
        scratch_shapes=[pltpu.VMEM((tm + V7X_SUBLANES, conv_cols), F32)],
        compiler_params=pltpu.CompilerParams(dimension_semantics=("arbitrary", "arbitrary"),
                                             vmem_limit_bytes=_vmem_limit(est + (8 << 20))),
        name="gdn_in",
    )(xn, w_qkvz, w_b, w_a, conv_w, alog, dtb)
    return outs


def _hi_lo(x):
    hi = x.astype(BF16)
    return hi, (x - hi.astype(F32)).astype(BF16)


def _dot2(a1, b1, a2, b2):
    return _dot(jnp.concatenate([a1, a2], axis=1), jnp.concatenate([b1, b2], axis=0))


def _unit_lower_inverse(a_list, eye):
    n = range(len(a_list))
    C = a_list[0].shape[0]
    p = [eye - a for a in a_list]
    x = [a.astype(BF16) for a in a_list]
    for _ in range(C.bit_length() - 2):
        x = [_dot(x[i], x[i]).astype(BF16) for i in n]
        p = [p[i] + _dot(p[i].astype(BF16), x[i]) for i in n]
    a_hi = [a.astype(BF16) for a in a_list]
    p_hl = [_hi_lo(t) for t in p]
    resid = [eye - p[i] - _dot2(a_hi[i], p_hl[i][0], a_hi[i], p_hl[i][1]) for i in n]
    return [p[i] + _dot(p_hl[i][0], resid[i].astype(BF16)) for i in n]


def _gdn_core_kernel(q_ref, k_ref, kb_ref, vb_ref, kbd_ref, qd_ref, kd_ref, z_ref, l_ref, cd_ref, nw_ref,
                     o_ref, s_ref):
    C = GDN_CHUNK
    n_batch = q_ref.shape[0]

    @pl.when(pl.program_id(0) == 0)
    def _():
        s_ref[...] = jnp.zeros_like(s_ref)

    row = lax.broadcasted_iota(jnp.int32, (C, C), 0)
    col = lax.broadcasted_iota(jnp.int32, (C, C), 1)
    strict = col < row
    eye = (col == row).astype(F32)
    chains = [(b, slice(h * GDN_DK, (h + 1) * GDN_DK)) for b in range(n_batch) for h in range(GDN_HEADS)]
    n = range(len(chains))
    kq = [_dot_nt(jnp.concatenate([kb_ref[b, :, hs], q_ref[b, :, hs]], axis=0), k_ref[b, :, hs])
          for b, hs in chains]
    a = [jnp.where(strict, kq[i][:C] * l_ref[b, :, hs], 0.0) for i, (b, hs) in enumerate(chains)]
    aqk = [(kq[i][C:] * l_ref[b, :, hs]).astype(BF16) for i, (b, hs) in enumerate(chains)]
    t = [_hi_lo(t_i) for t_i in _unit_lower_inverse(a, eye)]
    uw = []
    for i, (b, hs) in enumerate(chains):
        rhs = jnp.concatenate([vb_ref[b, :, hs], kbd_ref[b, :, hs]], axis=1)
        uw.append(_dot2(t[i][0], rhs, t[i][1], rhs))
    state = [s_ref[i] for i in n]
    sb = [st.astype(BF16) for st in state]
    vnb = [(uw[i][:, :GDN_DV] - _dot(uw[i][:, GDN_DV:].astype(BF16), sb[i])).astype(BF16) for i in n]
    o = [_dot2(qd_ref[b, :, hs], sb[i], aqk[i], vnb[i]) for i, (b, hs) in enumerate(chains)]
    new_state = [state[i] * cd_ref[b, :, hs] + _dot_tn(kd_ref[b, :, hs], vnb[i])
                 for i, (b, hs) in enumerate(chains)]
    for i, (b, hs) in enumerate(chains):
        s_ref[i] = new_state[i]
        on = o[i] * lax.rsqrt(jnp.mean(o[i] * o[i], axis=-1, keepdims=True) + NORM_EPS) * nw_ref[...]
        o_ref[b, :, hs] = (on * _silu(z_ref[b, :, hs].astype(F32))).astype(BF16)


def _gdn_core(ops, norm_w, j):
    q = ops[0]
    B, S, hw = q.shape
    C = GDN_CHUNK
    blk = pl.BlockSpec((B, C, hw), lambda i: (0, i, 0))
    est = 2 * (9 * B * C * hw * 2 + B * C * hw * 4) + 40 * B * GDN_HEADS * C * C * 4
    return pl.pallas_call(
        _gdn_core_kernel,
        grid=(S // C,),
        in_specs=[blk] * 9 + [pl.BlockSpec((B, None, 1, hw), lambda i: (0, i, 0, 0)),
                              _const_spec((None, 1, GDN_DV), lambda i: (j, 0, 0))],
        out_specs=blk,
        out_shape=jax.ShapeDtypeStruct((B, S, hw), BF16),
        scratch_shapes=[pltpu.VMEM((B * GDN_HEADS, GDN_DK, GDN_DV), F32)],
        compiler_params=pltpu.CompilerParams(dimension_semantics=("arbitrary",),
                                             vmem_limit_bytes=_vmem_limit(est)),
        name="gdn_core",
    )(*ops, norm_w)


def _pad_lanes(t):
    return jnp.pad(t, [(0, 0)] * (t.ndim - 1) + [(0, V7X_LANES - t.shape[-1])])


def kernel(x, positions, ffn1_norm, ffn1_w_gate_up, ffn1_w_down, mix_norm, ffn2_norm, ffn2_w_gate_up,
           ffn2_w_down, attn_w_in, attn_b_in, attn_sinks, attn_w_out, attn_b_out, gdn_w_in, gdn_conv_w,
           gdn_A_log, gdn_dt_bias, gdn_norm_w, gdn_w_out, final_norm):
    B, S, D = x.shape
    T = B * S
    depth = ffn1_norm.shape[0]
    assert S % ATTN_BLOCK == 0 and S % GDN_CHUNK == 0 and T % min(FFN_TM, T) == 0

    w1gu, w1d = ffn1_w_gate_up.astype(BF16), ffn1_w_down.astype(BF16)
    w2gu, w2d = ffn2_w_gate_up.astype(BF16), ffn2_w_down.astype(BF16)
    n1, nm, n2 = (t[:, None, :] for t in (ffn1_norm, mix_norm, ffn2_norm))
    a_w_in, a_w_out = attn_w_in.astype(BF16), attn_w_out.astype(BF16)
    a_b_in, a_b_out = attn_b_in[:, None, :], attn_b_out[:, None, :]
    conv_cols = gdn_conv_w.shape[-1]
    z_cols = GDN_HEADS * GDN_DV
    g_w_qkvz = gdn_w_in[:, :, :conv_cols + z_cols].astype(BF16)
    g_w_b = _pad_lanes(gdn_w_in[:, :, conv_cols + z_cols:conv_cols + z_cols + GDN_HEADS]).astype(BF16)
    g_w_a = _pad_lanes(gdn_w_in[:, :, conv_cols + z_cols + GDN_HEADS:]).astype(BF16)
    g_alog = _pad_lanes(gdn_A_log)[:, None, :]
    g_dtb = _pad_lanes(gdn_dt_bias)[:, None, :]
    g_nw = gdn_norm_w[:, None, :]
    g_w_out = gdn_w_out.astype(BF16)
    lane_d = jnp.arange(V7X_LANES) % ATTN_HEAD_DIM
    inv_freq = ROPE_THETA ** (-jnp.arange(0, ROPE_DIM, 2, dtype=F32) / ROPE_DIM)
    invf = jnp.where(lane_d < ROPE_DIM, inv_freq[lane_d % (ROPE_DIM // 2)], 0.0)[None, :].astype(F32)
    pos = positions.reshape(T, 1)

    h = x.reshape(T, D)
    for layer in range(depth):
        h, xn = _ffn(h, n1, w1gu, w1d, layer, next_norm_w=nm)
        j = layer // 2
        if layer % 2 == 0:
            q, krep, vrep = _attn_qkv(xn, pos, invf, a_w_in, a_b_in, j)
            o = _attn_core(q.reshape(B, S, -1), krep.reshape(B, S, -1), vrep.reshape(B, S, -1), attn_sinks[j])
            proj = (o.reshape(T, -1), a_w_out, j, a_b_out)
        else:
            ops = _gdn_in(xn.reshape(B, S, D), g_w_qkvz, g_w_b, g_w_a, gdn_conv_w, g_alog, g_dtb, j)
            o = _gdn_core(ops, g_nw, j)
            proj = (o.reshape(T, -1), g_w_out, j, None)
        last = layer == depth - 1
        h = _ffn(h, n2, w2gu, w2d, layer, proj=proj, final_w=final_norm[None, :] if last else None)
    return h.reshape(B, S, D)
```

```python
import functools

import jax
import jax.numpy as jnp
from jax import lax
from jax.experimental import pallas as pl
from jax.experimental.pallas import tpu as pltpu

F32 = jnp.float32
BF16 = jnp.bfloat16

NORM_EPS = 1e-6

ATTN_Q_HEADS = 16
ATTN_KV_HEADS = 4
ATTN_HEAD_DIM = 64
ATTN_GROUP = ATTN_Q_HEADS // ATTN_KV_HEADS
ATTN_BLOCK = 128
ROPE_DIM = ATTN_HEAD_DIM // 4
ROPE_THETA = 500000.0
GDN_HEADS = 8
GDN_DK = 128
GDN_DV = 128
GDN_CONV = 4
GDN_CHUNK = 128

V7X_LANES = 128
V7X_SUBLANES = 8
V7X_VMEM_BYTES = 64 * 1024 * 1024

FFN_TM = 1024
FFN_SUB = 256
FFN_FF_CHUNK = 256
QKV_TM = 1024
ATTN_TQ = 256
GDN_IN_TM = 512


def _vmem_limit(estimate_bytes):
    return int(min(V7X_VMEM_BYTES - (4 << 20), max(estimate_bytes, 16 << 20)))


def _dot(a, b):
    return jnp.dot(a, b, preferred_element_type=F32)


def _dot_nt(a, b):
    return lax.dot_general(a, b, (((1,), (1,)), ((), ())), preferred_element_type=F32)


def _dot_tn(a, b):
    return lax.dot_general(a, b, (((0,), (0,)), ((), ())), preferred_element_type=F32)


def _rms(x, w):
    ms = jnp.mean(x * x, axis=-1, keepdims=True)
    return x * lax.rsqrt(ms + NORM_EPS) * w


def _silu(x):
    return (0.5 * x) * (1.0 + jnp.tanh(0.5 * x))


def _const_spec(shape, index_map):
    return pl.BlockSpec(shape, index_map, pipeline_mode=pl.Buffered(1))


def _ffn_kernel(*refs, d_ff, ff_chunk, sub, has_proj, has_bias, has_final, has_next_norm):
    refs = list(refs)
    h_ref = refs.pop(0)
    if has_proj:
        o_ref = refs.pop(0)
        wo_ref = refs.pop(0)
    if has_bias:
        bo_ref = refs.pop(0)
    nw_ref = refs.pop(0)
    wgu_ref = refs.pop(0)
    wd_ref = refs.pop(0)
    if has_final:
        fw_ref = refs.pop(0)
    if has_next_norm:
        mw_ref = refs.pop(0)
        out_ref, xn_ref, act_ref = refs
    else:
        out_ref, act_ref = refs

    subs = [slice(r, r + sub) for r in range(0, h_ref.shape[0], sub)]
    xs, xns = [], []
    for rs in subs:
        x = h_ref[rs, :]
        if has_proj:
            x = x + _dot(o_ref[rs, :], wo_ref[...])
        if has_bias:
            x = x + bo_ref[...]
        xs.append(x)
        xns.append(_rms(x, nw_ref[...]).astype(BF16))
    for c in range(d_ff // ff_chunk):
        lo = c * ff_chunk
        for rs, xn in zip(subs, xns):
            g = _dot(xn, wgu_ref[:, lo:lo + ff_chunk])
            u = _dot(xn, wgu_ref[:, d_ff + lo:d_ff + lo + ff_chunk])
            act_ref[rs, lo:lo + ff_chunk] = (_silu(g) * u).astype(BF16)
    for rs, x in zip(subs, xs):
        y = x + 0.5 * _dot(act_ref[rs, :], wd_ref[...])
        if has_final:
            y = _rms(y, fw_ref[...])
        out_ref[rs, :] = y
        if has_next_norm:
            xn_ref[rs, :] = _rms(y, mw_ref[...]).astype(BF16)


def _ffn(h, norm_w, wgu, wd, layer, proj=None, final_w=None, next_norm_w=None):
    T, D = h.shape
    d_ff = wd.shape[1]
    tm = min(FFN_TM, T)
    ff_chunk = FFN_FF_CHUNK if d_ff % FFN_FF_CHUNK == 0 else d_ff
    args = [h]
    specs = [pl.BlockSpec((tm, D), lambda i: (i, 0))]
    est = 4 * tm * D * 4 + wgu.shape[1] * wgu.shape[2] * 2 + d_ff * D * 2 + tm * d_ff * 2 + 6 * tm * ff_chunk * 4
    has_proj = proj is not None
    has_bias = has_proj and proj[3] is not None
    if has_proj:
        o, w_out, j, bias = proj
        P = o.shape[1]
        args += [o, w_out]
        specs += [pl.BlockSpec((tm, P), lambda i: (i, 0)),
                  _const_spec((None, P, D), lambda i: (j, 0, 0))]
        est += 2 * tm * P * 2 + P * D * 2
        if has_bias:
            args.append(bias)
            specs.append(_const_spec((None, 1, D), lambda i: (j, 0, 0)))
    args += [norm_w, wgu, wd]
    specs += [_const_spec((None, 1, D), lambda i: (layer, 0, 0)),
              _const_spec((None, D, 2 * d_ff), lambda i: (layer, 0, 0)),
              _const_spec((None, d_ff, D), lambda i: (layer, 0, 0))]
    if final_w is not None:
        args.append(final_w)
        specs.append(_const_spec((1, D), lambda i: (0, 0)))
    tile = pl.BlockSpec((tm, D), lambda i: (i, 0))
    out_specs, out_shape = tile, jax.ShapeDtypeStruct((T, D), F32)
    if next_norm_w is not None:
        args.append(next_norm_w)
        specs.append(_const_spec((None, 1, D), lambda i: (layer, 0, 0)))
        out_specs, out_shape = [tile, tile], [out_shape, jax.ShapeDtypeStruct((T, D), BF16)]
        est += 2 * tm * D * 2
    kern = functools.partial(_ffn_kernel, d_ff=d_ff, ff_chunk=ff_chunk, sub=min(FFN_SUB, tm),
                             has_proj=has_proj, has_bias=has_bias,
                             has_final=final_w is not None, has_next_norm=next_norm_w is not None)
    return pl.pallas_call(
        kern,
        grid=(T // tm,),
        in_specs=specs,
        out_specs=out_specs,
        out_shape=out_shape,
        scratch_shapes=[pltpu.VMEM((tm, d_ff), BF16)],
        compiler_params=pltpu.CompilerParams(dimension_semantics=("parallel",),
                                             vmem_limit_bytes=_vmem_limit(est + (8 << 20))),
        name="ffn",
    )(*args)


def _attn_qkv_kernel(xn_ref, pos_ref, invf_ref, w_ref, b_ref, q_ref, k_ref, v_ref, *, q_w, kv_w):
    qkv = _dot(xn_ref[...], w_ref[...]) + b_ref[...]
    ang = pos_ref[...].astype(F32) * invf_ref[...]
    cos = jnp.cos(ang)
    sin = jnp.sin(ang)
    lane = lax.broadcasted_iota(jnp.int32, (1, V7X_LANES), 1)
    d = lane % ATTN_HEAD_DIM
    first_half = d < (ROPE_DIM // 2)
    sin_signed = jnp.where(first_half, -sin, sin)
    half = ROPE_DIM // 2

    def rope(t):
        up = pltpu.roll(t, V7X_LANES - half, axis=1)
        dn = pltpu.roll(t, half, axis=1)
        return t * cos + jnp.where(first_half, up, dn) * sin_signed

    scale = ATTN_HEAD_DIM ** -0.5
    for s in range(q_w // V7X_LANES):
        sl = slice(s * V7X_LANES, (s + 1) * V7X_LANES)
        q_ref[:, sl] = (rope(qkv[:, sl]) * scale).astype(BF16)
    low = lane < ATTN_HEAD_DIM
    for s in range(kv_w // V7X_LANES):
        ks = rope(qkv[:, q_w + s * V7X_LANES:q_w + (s + 1) * V7X_LANES])
        vs = qkv[:, q_w + kv_w + s * V7X_LANES:q_w + kv_w + (s + 1) * V7X_LANES]
        for t, ref in ((ks, k_ref), (vs, v_ref)):
            sw = pltpu.roll(t, ATTN_HEAD_DIM, axis=1)
            ref[:, (2 * s) * V7X_LANES:(2 * s + 1) * V7X_LANES] = jnp.where(low, t, sw).astype(BF16)
            ref[:, (2 * s + 1) * V7X_LANES:(2 * s + 2) * V7X_LANES] = jnp.where(low, sw, t).astype(BF16)


def _attn_qkv(xn, pos, invf, w_in, b_in, j):
    T, D = xn.shape
    q_w = ATTN_Q_HEADS * ATTN_HEAD_DIM
    kv_w = ATTN_KV_HEADS * ATTN_HEAD_DIM
    n_in = q_w + 2 * kv_w
    rep_w = ATTN_KV_HEADS * V7X_LANES
    tm = min(QKV_TM, T)
    est = 2 * tm * D * 2 + D * n_in * 2 + 3 * tm * n_in * 4 + 2 * tm * (q_w + 2 * rep_w) * 2
    kern = functools.partial(_attn_qkv_kernel, q_w=q_w, kv_w=kv_w)
    return pl.pallas_call(
        kern,
        grid=(T // tm,),
        in_specs=[pl.BlockSpec((tm, D), lambda i: (i, 0)),
                  pl.BlockSpec((tm, 1), lambda i: (i, 0)),
                  _const_spec((1, V7X_LANES), lambda i: (0, 0)),
                  _const_spec((None, D, n_in), lambda i: (j, 0, 0)),
                  _const_spec((None, 1, n_in), lambda i: (j, 0, 0))],
        out_specs=[pl.BlockSpec((tm, q_w), lambda i: (i, 0)),
                   pl.BlockSpec((tm, rep_w), lambda i: (i, 0)),
                   pl.BlockSpec((tm, rep_w), lambda i: (i, 0))],
        out_shape=[jax.ShapeDtypeStruct((T, q_w), BF16),
                   jax.ShapeDtypeStruct((T, rep_w), BF16),
                   jax.ShapeDtypeStruct((T, rep_w), BF16)],
        compiler_params=pltpu.CompilerParams(dimension_semantics=("parallel",),
                                             vmem_limit_bytes=_vmem_limit(est + (8 << 20))),
        name="attn_qkv",
    )(xn, pos, invf, w_in, b_in)


def _attn_core_kernel(sink_ref, q_ref, kc_ref, kp_ref, vc_ref, vp_ref, o_ref, *, tq):
    blk = ATTN_BLOCK
    first_tile = pl.program_id(1) == 0
    row = lax.broadcasted_iota(jnp.int32, (blk, blk), 0)
    col = lax.broadcasted_iota(jnp.int32, (blk, blk), 1)
    tril = col <= row
    lane = lax.broadcasted_iota(jnp.int32, (1, V7X_LANES), 1)
    half_mask = (lane < ATTN_HEAD_DIM, lane >= ATTN_HEAD_DIM)
    no_prev = jnp.where(first_tile, -jnp.inf, 0.0).astype(F32)
    n_blocks = tq // blk
    rows = [slice(j * blk, (j + 1) * blk) for j in range(n_blocks)]
    qsl = [slice(p * V7X_LANES, (p + 1) * V7X_LANES) for p in range(ATTN_Q_HEADS // 2)]
    km, vm = [], []
    for j in range(n_blocks):
        for h in range(ATTN_KV_HEADS):
            hs = slice(h * V7X_LANES, (h + 1) * V7X_LANES)
            kp = kp_ref[:, hs] if j == 0 else kc_ref[rows[j - 1], hs]
            vp = vp_ref[:, hs] if j == 0 else vc_ref[rows[j - 1], hs]
            kband = jnp.concatenate([kp, kc_ref[rows[j], hs]], axis=0)
            vband = jnp.concatenate([vp, vc_ref[rows[j], hs]], axis=0)
            km.append([jnp.where(m, kband, jnp.zeros_like(kband)) for m in half_mask])
            vm.append([jnp.where(m, vband, jnp.zeros_like(vband)) for m in half_mask])
    chains = [(j, n) for j in range(n_blocks) for n in range(ATTN_Q_HEADS)]
    s = [_dot_nt(q_ref[rows[j], qsl[n // 2]], km[j * ATTN_KV_HEADS + n // ATTN_GROUP][n % 2])
         for j, n in chains]
    pband = []
    for i, (j, n) in enumerate(chains):
        s_prev = s[i][:, :blk]
        if j == 0:
            s_prev = s_prev + no_prev
        sc = jnp.where(tril, s[i][:, blk:], s_prev)
        sink = sink_ref[n]
        m = jnp.maximum(jnp.max(sc, axis=-1, keepdims=True), sink)
        pr = jnp.exp(sc - m)
        denom = jnp.sum(pr, axis=-1, keepdims=True) + jnp.exp(sink - m)
        pn = pr * (1.0 / denom)
        zero = jnp.zeros_like(pn)
        pband.append(jnp.concatenate([jnp.where(tril, zero, pn), jnp.where(tril, pn, zero)],
                                     axis=1).astype(BF16))
    pv = [_dot(pband[i], vm[j * ATTN_KV_HEADS + n // ATTN_GROUP][n % 2]) for i, (j, n) in enumerate(chains)]
    for i in range(0, len(chains), 2):
        j, n = chains[i]
        o_ref[rows[j], qsl[n // 2]] = (pv[i] + pv[i + 1]).astype(BF16)


def _attn_core(q, krep, vrep, sinks):
    B, S, q_w = q.shape
    rep_w = krep.shape[-1]
    tq = min(ATTN_TQ, S)
    nb = tq // ATTN_BLOCK
    kern = functools.partial(_attn_core_kernel, tq=tq)
    cur = lambda b, i: (b, i, 0)
    prev = lambda b, i: (b, jnp.maximum(i * nb - 1, 0), 0)
    return pl.pallas_call(
        kern,
        grid=(B, S // tq),
        in_specs=[pl.BlockSpec(memory_space=pltpu.SMEM),
                  pl.BlockSpec((None, tq, q_w), cur),
                  pl.BlockSpec((None, tq, rep_w), cur),
                  pl.BlockSpec((None, ATTN_BLOCK, rep_w), prev),
                  pl.BlockSpec((None, tq, rep_w), cur),
                  pl.BlockSpec((None, ATTN_BLOCK, rep_w), prev)],
        out_specs=pl.BlockSpec((None, tq, q_w), cur),
        out_shape=jax.ShapeDtypeStruct((B, S, q_w), BF16),
        compiler_params=pltpu.CompilerParams(dimension_semantics=("parallel", "parallel")),
        name="attn_core",
    )(sinks, q, krep, krep, vrep, vrep)


def _split3(x):
    a = x.astype(BF16)
    r = x - a.astype(F32)
    b = r.astype(BF16)
    c = (r - b.astype(F32)).astype(BF16)
    return a, b, c


def _gdn_in_kernel(xn_ref, w_ref, wb_ref, wa_ref, cw_ref, alog_ref, dtb_ref,
                   q_ref, k_ref, kb_ref, vb_ref, kbd_ref, qd_ref, kd_ref, z_ref, l_ref, cd_ref,
                   buf_ref, *, tm):
    C = GDN_CHUNK
    qk_w = GDN_HEADS * GDN_DK
    conv_w = 2 * qk_w + GDN_HEADS * GDN_DV
    halo = V7X_SUBLANES

    @pl.when(pl.program_id(1) == 0)
    def _():
        buf_ref[0:halo, :] = jnp.zeros((halo, conv_w), F32)

    xn = xn_ref[...]
    buf_ref[halo:halo + tm, :] = _dot(xn, w_ref[:, :conv_w])
    z_ref[...] = _dot(xn, w_ref[:, conv_w:]).astype(BF16)
    beta = jax.nn.sigmoid(_dot(xn, wb_ref[...]))
    a = _dot(xn, wa_ref[...])
    g = -jnp.exp(alog_ref[...]) * jax.nn.softplus(a + dtb_ref[...])

    row = lax.broadcasted_iota(jnp.int32, (C, C), 0)
    col = lax.broadcasted_iota(jnp.int32, (C, C), 1)
    causal = col <= row
    tril = causal.astype(BF16)

    def conv_silu(lo):
        xf = buf_ref[:, lo:lo + V7X_LANES]
        acc = cw_ref[GDN_CONV - 1:GDN_CONV, lo:lo + V7X_LANES] * xf
        for s in range(1, GDN_CONV):
            acc = acc + cw_ref[GDN_CONV - 1 - s:GDN_CONV - s, lo:lo + V7X_LANES] * pltpu.roll(xf, s, axis=0)
        return _silu(acc[halo:halo + tm])

    def l2n(t):
        return t * lax.rsqrt(jnp.sum(t * t, axis=-1, keepdims=True) + NORM_EPS)

    d_cols, d_rows, e_cols, kd_cols = [], [], [], []
    for c in range(tm // C):
        g1, g2, g3 = _split3(g[c * C:(c + 1) * C, :])
        dc = _dot(tril, g1) + _dot(tril, g2) + _dot(tril, g3)
        d_cols.append(dc)
        d_rows.append(dc.T)
        e_cols.append(jnp.exp(dc))
        kd_cols.append(jnp.exp(dc[C - 1:C, :] - dc))

    heads = range(GDN_HEADS)
    q_act = [conv_silu(h * GDN_DK) for h in heads]
    k_act = [conv_silu(qk_w + h * GDN_DK) for h in heads]
    v_all = [conv_silu(2 * qk_w + h * GDN_DV) for h in heads]
    q_all = [l2n(t) * (GDN_DK ** -0.5) for t in q_act]
    k_all = [l2n(t) for t in k_act]
    for h in heads:
        hs = slice(h * GDN_DK, (h + 1) * GDN_DK)
        qh, kh, vh = q_all[h], k_all[h], v_all[h]
        for c in range(tm // C):
            rs = slice(c * C, (c + 1) * C)
            dB = jnp.broadcast_to(d_cols[c][:, h:h + 1], (C, C))
            bB = jnp.broadcast_to(beta[rs, h:h + 1], (C, C))
            eB = jnp.broadcast_to(e_cols[c][:, h:h + 1], (C, C))
            kdB = jnp.broadcast_to(kd_cols[c][:, h:h + 1], (C, C))
            dR = jnp.broadcast_to(d_rows[c][h:h + 1, :], (C, C))
            q_c, k_c, v_c = qh[rs], kh[rs], vh[rs]
            kb = k_c * bB
            q_ref[rs, hs] = q_c.astype(BF16)
            k_ref[rs, hs] = k_c.astype(BF16)
            kb_ref[rs, hs] = kb.astype(BF16)
            vb_ref[rs, hs] = (v_c * bB).astype(BF16)
            kbd_ref[rs, hs] = (kb * eB).astype(BF16)
            qd_ref[rs, hs] = (q_c * eB).astype(BF16)
            kd_ref[rs, hs] = (k_c * kdB).astype(BF16)
            l_ref[rs, hs] = jnp.exp(jnp.where(causal, dB - dR, -jnp.inf))
            cd_ref[c, :, hs] = eB[C - 1:C, :]

    buf_ref[0:halo, :] = buf_ref[tm:tm + halo, :]


def _gdn_in(xn, w_qkvz, w_b, w_a, conv_w, alog, dtb, j):
    B, S, D = xn.shape
    hw = GDN_HEADS * GDN_DK
    conv_cols = conv_w.shape[-1]
    tm = min(GDN_IN_TM, S)
    n_chunks = S // GDN_CHUNK
    tile = lambda b, i: (b, i, 0)
    big = pl.BlockSpec((None, tm, hw), tile)
    est = (2 * tm * D * 2 + D * w_qkvz.shape[-1] * 2 + 2 * D * V7X_LANES * 2 + (tm + 8) * conv_cols * 4
           + 2 * 8 * tm * hw * 2 + 2 * tm * hw * 4 + 3 * tm * hw * 4)
    kern = functools.partial(_gdn_in_kernel, tm=tm)
    outs = pl.pallas_call(
        kern,
        grid=(B, S // tm),
        in_specs=[pl.BlockSpec((None, tm, D), tile),
                  _const_spec((None, D, w_qkvz.shape[-1]), lambda b, i: (j, 0, 0)),
                  _const_spec((None, D, V7X_LANES), lambda b, i: (j, 0, 0)),
                  _const_spec((None, D, V7X_LANES), lambda b, i: (j, 0, 0)),
                  _const_spec((None, GDN_CONV, conv_cols), lambda b, i: (j, 0, 0)),
                  _const_spec((None, 1, V7X_LANES), lambda b, i: (j, 0, 0)),
                  _const_spec((None, 1, V7X_LANES), lambda b, i: (j, 0, 0))],
        out_specs=[big] * 8 + [pl.BlockSpec((None, tm, hw), tile),
                               pl.BlockSpec((None, tm // GDN_CHUNK, 1, hw), lambda b, i: (b, i, 0, 0))],
        out_shape=[jax.ShapeDtypeStruct((B, S, hw), BF16)] * 8
                  + [jax.ShapeDtypeStruct((B, S, hw), F32),
                     jax.ShapeDtypeStruct((B, n_chunks, 1, hw), F32)],
        scratch_shapes=[pltpu.VMEM((tm + V7X_SUBLANES, conv_cols), F32)],
        compiler_params=pltpu.CompilerParams(dimension_semantics=("arbitrary", "arbitrary"),
                                             vmem_limit_bytes=_vmem_limit(est + (8 << 20))),
        name="gdn_in",
    )(xn, w_qkvz, w_b, w_a, conv_w, alog, dtb)
    return outs


def _hi_lo(x):
    hi = x.astype(BF16)
    return hi, (x - hi.astype(F32)).astype(BF16)


def _dot2(a1, b1, a2, b2):
    return _dot(jnp.concatenate([a1, a2], axis=1), jnp.concatenate([b1, b2], axis=0))


def _unit_lower_inverse(a_list, eye):
    n = range(len(a_list))
    C = a_list[0].shape[0]
    row = lax.broadcasted_iota(jnp.int32, (C, C), 0)
    col = lax.broadcasted_iota(jnp.int32, (C, C), 1)
    same_block = (row // 2) == (col // 2)
    p = [eye - jnp.where(same_block, a, 0.0) for a in a_list]
    b = 2
    while b < C:
        joined = ((row // (2 * b)) == (col // (2 * b))) & ((row // b) != (col // b))
        off = [jnp.where(joined, a, 0.0).astype(BF16) for a in a_list]
        pb = [t.astype(BF16) for t in p]
        ta = [_dot(pb[i], off[i]).astype(BF16) for i in n]
        p = [p[i] - _dot(ta[i], pb[i]) for i in n]
        b *= 2
    a_hi = [a.astype(BF16) for a in a_list]
    p_hl = [_hi_lo(t) for t in p]
    resid = [eye - p[i] - _dot2(a_hi[i], p_hl[i][0], a_hi[i], p_hl[i][1]) for i in n]
    return [p[i] + _dot(p_hl[i][0], resid[i].astype(BF16)) for i in n]


def _gdn_core_kernel(q_ref, k_ref, kb_ref, vb_ref, kbd_ref, qd_ref, kd_ref, z_ref, l_ref, cd_ref, nw_ref,
                     o_ref, s_ref):
    C = GDN_CHUNK
    n_batch = q_ref.shape[0]

    @pl.when(pl.program_id(0) == 0)
    def _():
        s_ref[...] = jnp.zeros_like(s_ref)

    row = lax.broadcasted_iota(jnp.int32, (C, C), 0)
    col = lax.broadcasted_iota(jnp.int32, (C, C), 1)
    strict = col < row
    eye = (col == row).astype(F32)
    chains = [(b, slice(h * GDN_DK, (h + 1) * GDN_DK)) for b in range(n_batch) for h in range(GDN_HEADS)]
    n = range(len(chains))
    kq = [_dot_nt(jnp.concatenate([kb_ref[b, :, hs], q_ref[b, :, hs]], axis=0), k_ref[b, :, hs])
          for b, hs in chains]
    a = [jnp.where(strict, kq[i][:C] * l_ref[b, :, hs], 0.0) for i, (b, hs) in enumerate(chains)]
    aqk = [(kq[i][C:] * l_ref[b, :, hs]).astype(BF16) for i, (b, hs) in enumerate(chains)]
    t = [_hi_lo(t_i) for t_i in _unit_lower_inverse(a, eye)]
    uw = []
    for i, (b, hs) in enumerate(chains):
        rhs = jnp.concatenate([vb_ref[b, :, hs], kbd_ref[b, :, hs]], axis=1)
        uw.append(_dot2(t[i][0], rhs, t[i][1], rhs))
    state = [s_ref[i] for i in n]
    sb = [st.astype(BF16) for st in state]
    vnb = [(uw[i][:, :GDN_DV] - _dot(uw[i][:, GDN_DV:].astype(BF16), sb[i])).astype(BF16) for i in n]
    o = [_dot2(qd_ref[b, :, hs], sb[i], aqk[i], vnb[i]) for i, (b, hs) in enumerate(chains)]
    new_state = [state[i] * cd_ref[b, :, hs] + _dot_tn(kd_ref[b, :, hs], vnb[i])
                 for i, (b, hs) in enumerate(chains)]
    for i, (b, hs) in enumerate(chains):
        s_ref[i] = new_state[i]
        on = o[i] * lax.rsqrt(jnp.mean(o[i] * o[i], axis=-1, keepdims=True) + NORM_EPS) * nw_ref[...]
        o_ref[b, :, hs] = (on * _silu(z_ref[b, :, hs].astype(F32))).astype(BF16)


def _gdn_core(ops, norm_w, j):
    q = ops[0]
    B, S, hw = q.shape
    C = GDN_CHUNK
    blk = pl.BlockSpec((B, C, hw), lambda i: (0, i, 0))
    est = 2 * (9 * B * C * hw * 2 + B * C * hw * 4) + 40 * B * GDN_HEADS * C * C * 4
    return pl.pallas_call(
        _gdn_core_kernel,
        grid=(S // C,),
        in_specs=[blk] * 9 + [pl.BlockSpec((B, None, 1, hw), lambda i: (0, i, 0, 0)),
                              _const_spec((None, 1, GDN_DV), lambda i: (j, 0, 0))],
        out_specs=blk,
        out_shape=jax.ShapeDtypeStruct((B, S, hw), BF16),
        scratch_shapes=[pltpu.VMEM((B * GDN_HEADS, GDN_DK, GDN_DV), F32)],
        compiler_params=pltpu.CompilerParams(dimension_semantics=("arbitrary",),
                                             vmem_limit_bytes=_vmem_limit(est)),
        name="gdn_core",
    )(*ops, norm_w)


def _pad_lanes(t):
    return jnp.pad(t, [(0, 0)] * (t.ndim - 1) + [(0, V7X_LANES - t.shape[-1])])


def kernel(x, positions, ffn1_norm, ffn1_w_gate_up, ffn1_w_down, mix_norm, ffn2_norm, ffn2_w_gate_up,
           ffn2_w_down, attn_w_in, attn_b_in, attn_sinks, attn_w_out, attn_b_out, gdn_w_in, gdn_conv_w,
           gdn_A_log, gdn_dt_bias, gdn_norm_w, gdn_w_out, final_norm):
    B, S, D = x.shape
    T = B * S
    depth = ffn1_norm.shape[0]
    assert S % ATTN_BLOCK == 0 and S % GDN_CHUNK == 0 and T % min(FFN_TM, T) == 0

    w1gu, w1d = ffn1_w_gate_up.astype(BF16), ffn1_w_down.astype(BF16)
    w2gu, w2d = ffn2_w_gate_up.astype(BF16), ffn2_w_down.astype(BF16)
    n1, nm, n2 = (t[:, None, :] for t in (ffn1_norm, mix_norm, ffn2_norm))
    a_w_in, a_w_out = attn_w_in.astype(BF16), attn_w_out.astype(BF16)
    a_b_in, a_b_out = attn_b_in[:, None, :], attn_b_out[:, None, :]
    conv_cols = gdn_conv_w.shape[-1]
    z_cols = GDN_HEADS * GDN_DV
    g_w_qkvz = gdn_w_in[:, :, :conv_cols + z_cols].astype(BF16)
    g_w_b = _pad_lanes(gdn_w_in[:, :, conv_cols + z_cols:conv_cols + z_cols + GDN_HEADS]).astype(BF16)
    g_w_a = _pad_lanes(gdn_w_in[:, :, conv_cols + z_cols + GDN_HEADS:]).astype(BF16)
    g_alog = _pad_lanes(gdn_A_log)[:, None, :]
    g_dtb = _pad_lanes(gdn_dt_bias)[:, None, :]
    g_nw = gdn_norm_w[:, None, :]
    g_w_out = gdn_w_out.astype(BF16)
    lane_d = jnp.arange(V7X_LANES) % ATTN_HEAD_DIM
    inv_freq = ROPE_THETA ** (-jnp.arange(0, ROPE_DIM, 2, dtype=F32) / ROPE_DIM)
    invf = jnp.where(lane_d < ROPE_DIM, inv_freq[lane_d % (ROPE_DIM // 2)], 0.0)[None, :].astype(F32)
    pos = positions.reshape(T, 1)

    h = x.reshape(T, D)
    for layer in range(depth):
        h, xn = _ffn(h, n1, w1gu, w1d, layer, next_norm_w=nm)
        j = layer // 2
        if layer % 2 == 0:
            q, krep, vrep = _attn_qkv(xn, pos, invf, a_w_in, a_b_in, j)
            o = _attn_core(q.reshape(B, S, -1), krep.reshape(B, S, -1), vrep.reshape(B, S, -1), attn_sinks[j])
            proj = (o.reshape(T, -1), a_w_out, j, a_b_out)
        else:
            ops = _gdn_in(xn.reshape(B, S, D), g_w_qkvz, g_w_b, g_w_a, gdn_conv_w, g_alog, g_dtb, j)
            o = _gdn_core(ops, g_nw, j)
            proj = (o.reshape(T, -1), g_w_out, j, None)
        last = layer == depth - 1
        h = _ffn(h, n2, w2gu, w2d, layer, proj=proj, final_w=final_norm[None, :] if last else None)
    return h.reshape(B, S, D)
```

```python
import functools

import jax
import jax.numpy as jnp
from jax import lax
from jax.experimental import pallas as pl
from jax.experimental.pallas import tpu as pltpu

F32 = jnp.float32
BF16 = jnp.bfloat16

NORM_EPS = 1e-6

ATTN_Q_HEADS = 16
ATTN_KV_HEADS = 4
ATTN_HEAD_DIM = 64
ATTN_GROUP = ATTN_Q_HEADS // ATTN_KV_HEADS
ATTN_BLOCK = 128
ROPE_DIM = ATTN_HEAD_DIM // 4
ROPE_THETA = 500000.0
GDN_HEADS = 8
GDN_DK = 128
GDN_DV = 128
GDN_CONV = 4
GDN_CHUNK = 128

V7X_LANES = 128
V7X_SUBLANES = 8
V7X_VMEM_BYTES = 64 * 1024 * 1024

FFN_TM = 1024
FFN_SUB = 256
FFN_FF_CHUNK = 256
QKV_TM = 1024
ATTN_TQ = 256
GDN_IN_TM = 512


def _vmem_limit(estimate_bytes):
    return int(min(V7X_VMEM_BYTES - (4 << 20), max(estimate_bytes, 16 << 20)))


def _dot(a, b):
    return jnp.dot(a, b, preferred_element_type=F32)


def _dot_nt(a, b):
    return lax.dot_general(a, b, (((1,), (1,)), ((), ())), preferred_element_type=F32)


def _dot_tn(a, b):
    return lax.dot_general(a, b, (((0,), (0,)), ((), ())), preferred_element_type=F32)


def _rms(x, w):
    ms = jnp.mean(x * x, axis=-1, keepdims=True)
    return x * lax.rsqrt(ms + NORM_EPS) * w


def _silu(x):
    return (0.5 * x) * (1.0 + jnp.tanh(0.5 * x))


def _const_spec(shape, index_map):
    return pl.BlockSpec(shape, index_map, pipeline_mode=pl.Buffered(1))


def _ffn_kernel(*refs, d_ff, ff_chunk, sub, has_proj, has_bias, has_final, has_next_norm):
    refs = list(refs)
    h_ref = refs.pop(0)
    if has_proj:
        o_ref = refs.pop(0)
        wo_ref = refs.pop(0)
    if has_bias:
        bo_ref = refs.pop(0)
    nw_ref = refs.pop(0)
    wgu_ref = refs.pop(0)
    wd_ref = refs.pop(0)
    if has_final:
        fw_ref = refs.pop(0)
    if has_next_norm:
        mw_ref = refs.pop(0)
        out_ref, xn_ref, act_ref = refs
    else:
        out_ref, act_ref = refs

    subs = [slice(r, r + sub) for r in range(0, h_ref.shape[0], sub)]
    xs, xns = [], []
    for rs in subs:
        x = h_ref[rs, :]
        if has_proj:
            x = x + _dot(o_ref[rs, :], wo_ref[...])
        if has_bias:
            x = x + bo_ref[...]
        xs.append(x)
        xns.append(_rms(x, nw_ref[...]).astype(BF16))
    for c in range(d_ff // ff_chunk):
        lo = c * ff_chunk
        for rs, xn in zip(subs, xns):
            g = _dot(xn, wgu_ref[:, lo:lo + ff_chunk])
            u = _dot(xn, wgu_ref[:, d_ff + lo:d_ff + lo + ff_chunk])
            act_ref[rs, lo:lo + ff_chunk] = (_silu(g) * u).astype(BF16)
    for rs, x in zip(subs, xs):
        y = x + 0.5 * _dot(act_ref[rs, :], wd_ref[...])
        if has_final:
            y = _rms(y, fw_ref[...])
        out_ref[rs, :] = y
        if has_next_norm:
            xn_ref[rs, :] = _rms(y, mw_ref[...]).astype(BF16)


def _ffn(h, norm_w, wgu, wd, layer, proj=None, final_w=None, next_norm_w=None):
    T, D = h.shape
    d_ff = wd.shape[1]
    tm = min(FFN_TM, T)
    ff_chunk = FFN_FF_CHUNK if d_ff % FFN_FF_CHUNK == 0 else d_ff
    args = [h]
    specs = [pl.BlockSpec((tm, D), lambda i: (i, 0))]
    est = 4 * tm * D * 4 + wgu.shape[1] * wgu.shape[2] * 2 + d_ff * D * 2 + tm * d_ff * 2 + 6 * tm * ff_chunk * 4
    has_proj = proj is not None
    has_bias = has_proj and proj[3] is not None
    if has_proj:
        o, w_out, j, bias = proj
        P = o.shape[1]
        args += [o, w_out]
        specs += [pl.BlockSpec((tm, P), lambda i: (i, 0)),
                  _const_spec((None, P, D), lambda i: (j, 0, 0))]
        est += 2 * tm * P * 2 + P * D * 2
        if has_bias:
            args.append(bias)
            specs.append(_const_spec((None, 1, D), lambda i: (j, 0, 0)))
    args += [norm_w, wgu, wd]
    specs += [_const_spec((None, 1, D), lambda i: (layer, 0, 0)),
              _const_spec((None, D, 2 * d_ff), lambda i: (layer, 0, 0)),
              _const_spec((None, d_ff, D), lambda i: (layer, 0, 0))]
    if final_w is not None:
        args.append(final_w)
        specs.append(_const_spec((1, D), lambda i: (0, 0)))
    tile = pl.BlockSpec((tm, D), lambda i: (i, 0))
    out_specs, out_shape = tile, jax.ShapeDtypeStruct((T, D), F32)
    if next_norm_w is not None:
        args.append(next_norm_w)
        specs.append(_const_spec((None, 1, D), lambda i: (layer, 0, 0)))
        out_specs, out_shape = [tile, tile], [out_shape, jax.ShapeDtypeStruct((T, D), BF16)]
        est += 2 * tm * D * 2
    kern = functools.partial(_ffn_kernel, d_ff=d_ff, ff_chunk=ff_chunk, sub=min(FFN_SUB, tm),
                             has_proj=has_proj, has_bias=has_bias,
                             has_final=final_w is not None, has_next_norm=next_norm_w is not None)
    return pl.pallas_call(
        kern,
        grid=(T // tm,),
        in_specs=specs,
        out_specs=out_specs,
        out_shape=out_shape,
        scratch_shapes=[pltpu.VMEM((tm, d_ff), BF16)],
        compiler_params=pltpu.CompilerParams(dimension_semantics=("parallel",),
                                             vmem_limit_bytes=_vmem_limit(est + (8 << 20))),
        name="ffn",
    )(*args)


def _attn_qkv_kernel(xn_ref, pos_ref, invf_ref, w_ref, b_ref, q_ref, k_ref, v_ref, *, q_w, kv_w):
    qkv = _dot(xn_ref[...], w_ref[...]) + b_ref[...]
    ang = pos_ref[...].astype(F32) * invf_ref[...]
    cos = jnp.cos(ang)
    sin = jnp.sin(ang)
    lane = lax.broadcasted_iota(jnp.int32, (1, V7X_LANES), 1)
    d = lane % ATTN_HEAD_DIM
    first_half = d < (ROPE_DIM // 2)
    sin_signed = jnp.where(first_half, -sin, sin)
    half = ROPE_DIM // 2

    def rope(t):
        up = pltpu.roll(t, V7X_LANES - half, axis=1)
        dn = pltpu.roll(t, half, axis=1)
        return t * cos + jnp.where(first_half, up, dn) * sin_signed

    scale = ATTN_HEAD_DIM ** -0.5
    for s in range(q_w // V7X_LANES):
        sl = slice(s * V7X_LANES, (s + 1) * V7X_LANES)
        q_ref[:, sl] = (rope(qkv[:, sl]) * scale).astype(BF16)
    low = lane < ATTN_HEAD_DIM
    for s in range(kv_w // V7X_LANES):
        ks = rope(qkv[:, q_w + s * V7X_LANES:q_w + (s + 1) * V7X_LANES])
        vs = qkv[:, q_w + kv_w + s * V7X_LANES:q_w + kv_w + (s + 1) * V7X_LANES]
        for t, ref in ((ks, k_ref), (vs, v_ref)):
            sw = pltpu.roll(t, ATTN_HEAD_DIM, axis=1)
            ref[:, (2 * s) * V7X_LANES:(2 * s + 1) * V7X_LANES] = jnp.where(low, t, sw).astype(BF16)
            ref[:, (2 * s + 1) * V7X_LANES:(2 * s + 2) * V7X_LANES] = jnp.where(low, sw, t).astype(BF16)


def _attn_qkv(xn, pos, invf, w_in, b_in, j):
    T, D = xn.shape
    q_w = ATTN_Q_HEADS * ATTN_HEAD_DIM
    kv_w = ATTN_KV_HEADS * ATTN_HEAD_DIM
    n_in = q_w + 2 * kv_w
    rep_w = ATTN_KV_HEADS * V7X_LANES
    tm = min(QKV_TM, T)
    est = 2 * tm * D * 2 + D * n_in * 2 + 3 * tm * n_in * 4 + 2 * tm * (q_w + 2 * rep_w) * 2
    kern = functools.partial(_attn_qkv_kernel, q_w=q_w, kv_w=kv_w)
    return pl.pallas_call(
        kern,
        grid=(T // tm,),
        in_specs=[pl.BlockSpec((tm, D), lambda i: (i, 0)),
                  pl.BlockSpec((tm, 1), lambda i: (i, 0)),
                  _const_spec((1, V7X_LANES), lambda i: (0, 0)),
                  _const_spec((None, D, n_in), lambda i: (j, 0, 0)),
                  _const_spec((None, 1, n_in), lambda i: (j, 0, 0))],
        out_specs=[pl.BlockSpec((tm, q_w), lambda i: (i, 0)),
                   pl.BlockSpec((tm, rep_w), lambda i: (i, 0)),
                   pl.BlockSpec((tm, rep_w), lambda i: (i, 0))],
        out_shape=[jax.ShapeDtypeStruct((T, q_w), BF16),
                   jax.ShapeDtypeStruct((T, rep_w), BF16),
                   jax.ShapeDtypeStruct((T, rep_w), BF16)],
        compiler_params=pltpu.CompilerParams(dimension_semantics=("parallel",),
                                             vmem_limit_bytes=_vmem_limit(est + (8 << 20))),
        name="attn_qkv",
    )(xn, pos, invf, w_in, b_in)


def _attn_core_kernel(sink_ref, q_ref, kc_ref, kp_ref, vc_ref, vp_ref, o_ref, *, tq):
    blk = ATTN_BLOCK
    first_tile = pl.program_id(1) == 0
    row = lax.broadcasted_iota(jnp.int32, (blk, blk), 0)
    col = lax.broadcasted_iota(jnp.int32, (blk, blk), 1)
    tril = col <= row
    lane = lax.broadcasted_iota(jnp.int32, (1, V7X_LANES), 1)
    half_mask = (lane < ATTN_HEAD_DIM, lane >= ATTN_HEAD_DIM)
    no_prev = jnp.where(first_tile, -jnp.inf, 0.0).astype(F32)
    n_blocks = tq // blk
    rows = [slice(j * blk, (j + 1) * blk) for j in range(n_blocks)]
    qsl = [slice(p * V7X_LANES, (p + 1) * V7X_LANES) for p in range(ATTN_Q_HEADS // 2)]
    km, vm = [], []
    for j in range(n_blocks):
        for h in range(ATTN_KV_HEADS):
            hs = slice(h * V7X_LANES, (h + 1) * V7X_LANES)
            kp = kp_ref[:, hs] if j == 0 else kc_ref[rows[j - 1], hs]
            vp = vp_ref[:, hs] if j == 0 else vc_ref[rows[j - 1], hs]
            kband = jnp.concatenate([kp, kc_ref[rows[j], hs]], axis=0)
            vband = jnp.concatenate([vp, vc_ref[rows[j], hs]], axis=0)
            km.append([jnp.where(m, kband, jnp.zeros_like(kband)) for m in half_mask])
            vm.append([jnp.where(m, vband, jnp.zeros_like(vband)) for m in half_mask])
    chains = [(j, n) for j in range(n_blocks) for n in range(ATTN_Q_HEADS)]
    s = [_dot_nt(q_ref[rows[j], qsl[n // 2]], km[j * ATTN_KV_HEADS + n // ATTN_GROUP][n % 2])
         for j, n in chains]
    pband = []
    for i, (j, n) in enumerate(chains):
        s_prev = s[i][:, :blk]
        if j == 0:
            s_prev = s_prev + no_prev
        sc = jnp.where(tril, s[i][:, blk:], s_prev)
        sink = sink_ref[n]
        m = jnp.maximum(jnp.max(sc, axis=-1, keepdims=True), sink)
        pr = jnp.exp(sc - m)
        denom = jnp.sum(pr, axis=-1, keepdims=True) + jnp.exp(sink - m)
        pn = pr * (1.0 / denom)
        zero = jnp.zeros_like(pn)
        pband.append(jnp.concatenate([jnp.where(tril, zero, pn), jnp.where(tril, pn, zero)],
                                     axis=1).astype(BF16))
    for i in range(0, len(chains), 2):
        j, n = chains[i]
        vmh = vm[j * ATTN_KV_HEADS + n // ATTN_GROUP]
        pv = _dot(jnp.concatenate([pband[i], pband[i + 1]], axis=1), jnp.concatenate([vmh[0], vmh[1]], axis=0))
        o_ref[rows[j], qsl[n // 2]] = pv.astype(BF16)


def _attn_core(q, krep, vrep, sinks):
    B, S, q_w = q.shape
    rep_w = krep.shape[-1]
    tq = min(ATTN_TQ, S)
    nb = tq // ATTN_BLOCK
    kern = functools.partial(_attn_core_kernel, tq=tq)
    cur = lambda b, i: (b, i, 0)
    prev = lambda b, i: (b, jnp.maximum(i * nb - 1, 0), 0)
    return pl.pallas_call(
        kern,
        grid=(B, S // tq),
        in_specs=[pl.BlockSpec(memory_space=pltpu.SMEM),
                  pl.BlockSpec((None, tq, q_w), cur),
                  pl.BlockSpec((None, tq, rep_w), cur),
                  pl.BlockSpec((None, ATTN_BLOCK, rep_w), prev),
                  pl.BlockSpec((None, tq, rep_w), cur),
                  pl.BlockSpec((None, ATTN_BLOCK, rep_w), prev)],
        out_specs=pl.BlockSpec((None, tq, q_w), cur),
        out_shape=jax.ShapeDtypeStruct((B, S, q_w), BF16),
        compiler_params=pltpu.CompilerParams(dimension_semantics=("parallel", "parallel")),
        name="attn_core",
    )(sinks, q, krep, krep, vrep, vrep)


def _split3(x):
    a = x.astype(BF16)
    r = x - a.astype(F32)
    b = r.astype(BF16)
    c = (r - b.astype(F32)).astype(BF16)
    return a, b, c


def _gdn_in_kernel(xn_ref, w_ref, wb_ref, wa_ref, cw_ref, alog_ref, dtb_ref,
                   q_ref, k_ref, kb_ref, vb_ref, kbd_ref, qd_ref, kd_ref, z_ref, l_ref, cd_ref,
                   buf_ref, *, tm):
    C = GDN_CHUNK
    qk_w = GDN_HEADS * GDN_DK
    conv_w = 2 * qk_w + GDN_HEADS * GDN_DV
    halo = V7X_SUBLANES

    @pl.when(pl.program_id(1) == 0)
    def _():
        buf_ref[0:halo, :] = jnp.zeros((halo, conv_w), F32)

    xn = xn_ref[...]
    buf_ref[halo:halo + tm, :] = _dot(xn, w_ref[:, :conv_w])
    z_ref[...] = _dot(xn, w_ref[:, conv_w:]).astype(BF16)
    beta = jax.nn.sigmoid(_dot(xn, wb_ref[...]))
    a = _dot(xn, wa_ref[...])
    g = -jnp.exp(alog_ref[...]) * jax.nn.softplus(a + dtb_ref[...])

    row = lax.broadcasted_iota(jnp.int32, (C, C), 0)
    col = lax.broadcasted_iota(jnp.int32, (C, C), 1)
    causal = col <= row
    tril = causal.astype(BF16)

    def conv_silu(lo):
        xf = buf_ref[:, lo:lo + V7X_LANES]
        acc = cw_ref[GDN_CONV - 1:GDN_CONV, lo:lo + V7X_LANES] * xf
        for s in range(1, GDN_CONV):
            acc = acc + cw_ref[GDN_CONV - 1 - s:GDN_CONV - s, lo:lo + V7X_LANES] * pltpu.roll(xf, s, axis=0)
        return _silu(acc[halo:halo + tm])

    def l2n(t):
        return t * lax.rsqrt(jnp.sum(t * t, axis=-1, keepdims=True) + NORM_EPS)

    d_cols, d_rows, e_cols, kd_cols = [], [], [], []
    for c in range(tm // C):
        g1, g2, g3 = _split3(g[c * C:(c + 1) * C, :])
        dc = _dot(tril, g1) + _dot(tril, g2) + _dot(tril, g3)
        d_cols.append(dc)
        d_rows.append(dc.T)
        e_cols.append(jnp.exp(dc))
        kd_cols.append(jnp.exp(dc[C - 1:C, :] - dc))

    heads = range(GDN_HEADS)
    q_act = [conv_silu(h * GDN_DK) for h in heads]
    k_act = [conv_silu(qk_w + h * GDN_DK) for h in heads]
    v_all = [conv_silu(2 * qk_w + h * GDN_DV) for h in heads]
    q_all = [l2n(t) * (GDN_DK ** -0.5) for t in q_act]
    k_all = [l2n(t) for t in k_act]
    for h in heads:
        hs = slice(h * GDN_DK, (h + 1) * GDN_DK)
        qh, kh, vh = q_all[h], k_all[h], v_all[h]
        for c in range(tm // C):
            rs = slice(c * C, (c + 1) * C)
            dB = jnp.broadcast_to(d_cols[c][:, h:h + 1], (C, C))
            bB = jnp.broadcast_to(beta[rs, h:h + 1], (C, C))
            eB = jnp.broadcast_to(e_cols[c][:, h:h + 1], (C, C))
            kdB = jnp.broadcast_to(kd_cols[c][:, h:h + 1], (C, C))
            dR = jnp.broadcast_to(d_rows[c][h:h + 1, :], (C, C))
            q_c, k_c, v_c = qh[rs], kh[rs], vh[rs]
            kb = k_c * bB
            q_ref[rs, hs] = q_c.astype(BF16)
            k_ref[rs, hs] = k_c.astype(BF16)
            kb_ref[rs, hs] = kb.astype(BF16)
            vb_ref[rs, hs] = (v_c * bB).astype(BF16)
            kbd_ref[rs, hs] = (kb * eB).astype(BF16)
            qd_ref[rs, hs] = (q_c * eB).astype(BF16)
            kd_ref[rs, hs] = (k_c * kdB).astype(BF16)
            l_ref[rs, hs] = jnp.exp(jnp.where(causal, dB - dR, -jnp.inf))
            cd_ref[c, :, hs] = eB[C - 1:C, :]

    buf_ref[0:halo, :] = buf_ref[tm:tm + halo, :]


def _gdn_in(xn, w_qkvz, w_b, w_a, conv_w, alog, dtb, j):
    B, S, D = xn.shape
    hw = GDN_HEADS * GDN_DK
    conv_cols = conv_w.shape[-1]
    tm = min(GDN_IN_TM, S)
    n_chunks = S // GDN_CHUNK
    tile = lambda b, i: (b, i, 0)
    big = pl.BlockSpec((None, tm, hw), tile)
    est = (2 * tm * D * 2 + D * w_qkvz.shape[-1] * 2 + 2 * D * V7X_LANES * 2 + (tm + 8) * conv_cols * 4
           + 2 * 8 * tm * hw * 2 + 2 * tm * hw * 4 + 3 * tm * hw * 4)
    kern = functools.partial(_gdn_in_kernel, tm=tm)
    outs = pl.pallas_call(
        kern,
        grid=(B, S // tm),
        in_specs=[pl.BlockSpec((None, tm, D), tile),
                  _const_spec((None, D, w_qkvz.shape[-1]), lambda b, i: (j, 0, 0)),
                  _const_spec((None, D, V7X_LANES), lambda b, i: (j, 0, 0)),
                  _const_spec((None, D, V7X_LANES), lambda b, i: (j, 0, 0)),
                  _const_spec((None, GDN_CONV, conv_cols), lambda b, i: (j, 0, 0)),
                  _const_spec((None, 1, V7X_LANES), lambda b, i: (j, 0, 0)),
                  _const_spec((None, 1, V7X_LANES), lambda b, i: (j, 0, 0))],
        out_specs=[big] * 8 + [pl.BlockSpec((None, tm, hw), tile),
                               pl.BlockSpec((None, tm // GDN_CHUNK, 1, hw), lambda b, i: (b, i, 0, 0))],
        out_shape=[jax.ShapeDtypeStruct((B, S, hw), BF16)] * 8
                  + [jax.ShapeDtypeStruct((B, S, hw), F32),
                     jax.ShapeDtypeStruct((B, n_chunks, 1, hw), F32)],
        scratch_shapes=[pltpu.VMEM((tm + V7X_SUBLANES, conv_cols), F32)],
        compiler_params=pltpu.CompilerParams(dimension_semantics=("arbitrary", "arbitrary"),
                                             vmem_limit_bytes=_vmem_limit(est + (8 << 20))),
        name="gdn_in",
    )(xn, w_qkvz, w_b, w_a, conv_w, alog, dtb)
    return outs


def _hi_lo(x):
    hi = x.astype(BF16)
    return hi, (x - hi.astype(F32)).astype(BF16)


def _dot2(a1, b1, a2, b2):
    return _dot(jnp.concatenate([a1, a2], axis=1), jnp.concatenate([b1, b2], axis=0))


def _unit_lower_inverse(a_list, eye):
    n = range(len(a_list))
    C = a_list[0].shape[0]
    row = lax.broadcasted_iota(jnp.int32, (C, C), 0)
    col = lax.broadcasted_iota(jnp.int32, (C, C), 1)
    same_block = (row // 2) == (col // 2)
    p = [eye - jnp.where(same_block, a, 0.0) for a in a_list]
    b = 2
    while b < C:
        joined = ((row // (2 * b)) == (col // (2 * b))) & ((row // b) != (col // b))
        off = [jnp.where(joined, a, 0.0).astype(BF16) for a in a_list]
        pb = [t.astype(BF16) for t in p]
        ta = [_dot(pb[i], off[i]).astype(BF16) for i in n]
        p = [p[i] - _dot(ta[i], pb[i]) for i in n]
        b *= 2
    a_hi = [a.astype(BF16) for a in a_list]
    p_hl = [_hi_lo(t) for t in p]
    resid = [eye - p[i] - _dot2(a_hi[i], p_hl[i][0], a_hi[i], p_hl[i][1]) for i in n]
    return [p[i] + _dot(p_hl[i][0], resid[i].astype(BF16)) for i in n]


def _gdn_core_kernel(q_ref, k_ref, kb_ref, vb_ref, kbd_ref, qd_ref, kd_ref, z_ref, l_ref, cd_ref, nw_ref,
                     o_ref, s_ref):
    C = GDN_CHUNK
    n_batch = q_ref.shape[0]

    @pl.when(pl.program_id(0) == 0)
    def _():
        s_ref[...] = jnp.zeros_like(s_ref)

    row = lax.broadcasted_iota(jnp.int32, (C, C), 0)
    col = lax.broadcasted_iota(jnp.int32, (C, C), 1)
    strict = col < row
    eye = (col == row).astype(F32)
    chains = [(b, slice(h * GDN_DK, (h + 1) * GDN_DK)) for b in range(n_batch) for h in range(GDN_HEADS)]
    n = range(len(chains))
    kq = [_dot_nt(jnp.concatenate([kb_ref[b, :, hs], q_ref[b, :, hs]], axis=0), k_ref[b, :, hs])
          for b, hs in chains]
    a = [jnp.where(strict, kq[i][:C] * l_ref[b, :, hs], 0.0) for i, (b, hs) in enumerate(chains)]
    aqk = [(kq[i][C:] * l_ref[b, :, hs]).astype(BF16) for i, (b, hs) in enumerate(chains)]
    t = [_hi_lo(t_i) for t_i in _unit_lower_inverse(a, eye)]
    uw = []
    for i, (b, hs) in enumerate(chains):
        rhs = jnp.concatenate([vb_ref[b, :, hs], kbd_ref[b, :, hs]], axis=1)
        uw.append(_dot2(t[i][0], rhs, t[i][1], rhs))
    state = [s_ref[i] for i in n]
    sb = [st.astype(BF16) for st in state]
    vnb = [(uw[i][:, :GDN_DV] - _dot(uw[i][:, GDN_DV:].astype(BF16), sb[i])).astype(BF16) for i in n]
    o = [_dot2(qd_ref[b, :, hs], sb[i], aqk[i], vnb[i]) for i, (b, hs) in enumerate(chains)]
    new_state = [state[i] * cd_ref[b, :, hs] + _dot_tn(kd_ref[b, :, hs], vnb[i])
                 for i, (b, hs) in enumerate(chains)]
    for i, (b, hs) in enumerate(chains):
        s_ref[i] = new_state[i]
        on = o[i] * lax.rsqrt(jnp.mean(o[i] * o[i], axis=-1, keepdims=True) + NORM_EPS) * nw_ref[...]
        o_ref[b, :, hs] = (on * _silu(z_ref[b, :, hs].astype(F32))).astype(BF16)


def _gdn_core(ops, norm_w, j):
    q = ops[0]
    B, S, hw = q.shape
    C = GDN_CHUNK
    blk = pl.BlockSpec((B, C, hw), lambda i: (0, i, 0))
    est = 2 * (9 * B * C * hw * 2 + B * C * hw * 4) + 40 * B * GDN_HEADS * C * C * 4
    return pl.pallas_call(
        _gdn_core_kernel,
        grid=(S // C,),
        in_specs=[blk] * 9 + [pl.BlockSpec((B, None, 1, hw), lambda i: (0, i, 0, 0)),
                              _const_spec((None, 1, GDN_DV), lambda i: (j, 0, 0))],
        out_specs=blk,
        out_shape=jax.ShapeDtypeStruct((B, S, hw), BF16),
        scratch_shapes=[pltpu.VMEM((B * GDN_HEADS, GDN_DK, GDN_DV), F32)],
        compiler_params=pltpu.CompilerParams(dimension_semantics=("arbitrary",),
                                             vmem_limit_bytes=_vmem_limit(est)),
        name="gdn_core",
    )(*ops, norm_w)


def _pad_lanes(t):
    return jnp.pad(t, [(0, 0)] * (t.ndim - 1) + [(0, V7X_LANES - t.shape[-1])])


def kernel(x, positions, ffn1_norm, ffn1_w_gate_up, ffn1_w_down, mix_norm, ffn2_norm, ffn2_w_gate_up,
           ffn2_w_down, attn_w_in, attn_b_in, attn_sinks, attn_w_out, attn_b_out, gdn_w_in, gdn_conv_w,
           gdn_A_log, gdn_dt_bias, gdn_norm_w, gdn_w_out, final_norm):
    B, S, D = x.shape
    T = B * S
    depth = ffn1_norm.shape[0]
    assert S % ATTN_BLOCK == 0 and S % GDN_CHUNK == 0 and T % min(FFN_TM, T) == 0

    w1gu, w1d = ffn1_w_gate_up.astype(BF16), ffn1_w_down.astype(BF16)
    w2gu, w2d = ffn2_w_gate_up.astype(BF16), ffn2_w_down.astype(BF16)
    n1, nm, n2 = (t[:, None, :] for t in (ffn1_norm, mix_norm, ffn2_norm))
    a_w_in, a_w_out = attn_w_in.astype(BF16), attn_w_out.astype(BF16)
    a_b_in, a_b_out = attn_b_in[:, None, :], attn_b_out[:, None, :]
    conv_cols = gdn_conv_w.shape[-1]
    z_cols = GDN_HEADS * GDN_DV
    g_w_qkvz = gdn_w_in[:, :, :conv_cols + z_cols].astype(BF16)
    g_w_b = _pad_lanes(gdn_w_in[:, :, conv_cols + z_cols:conv_cols + z_cols + GDN_HEADS]).astype(BF16)
    g_w_a = _pad_lanes(gdn_w_in[:, :, conv_cols + z_cols + GDN_HEADS:]).astype(BF16)
    g_alog = _pad_lanes(gdn_A_log)[:, None, :]
    g_dtb = _pad_lanes(gdn_dt_bias)[:, None, :]
    g_nw = gdn_norm_w[:, None, :]
    g_w_out = gdn_w_out.astype(BF16)
    lane_d = jnp.arange(V7X_LANES) % ATTN_HEAD_DIM
    inv_freq = ROPE_THETA ** (-jnp.arange(0, ROPE_DIM, 2, dtype=F32) / ROPE_DIM)
    invf = jnp.where(lane_d < ROPE_DIM, inv_freq[lane_d % (ROPE_DIM // 2)], 0.0)[None, :].astype(F32)
    pos = positions.reshape(T, 1)

    h = x.reshape(T, D)
    for layer in range(depth):
        h, xn = _ffn(h, n1, w1gu, w1d, layer, next_norm_w=nm)
        j = layer // 2
        if layer % 2 == 0:
            q, krep, vrep = _attn_qkv(xn, pos, invf, a_w_in, a_b_in, j)
            o = _attn_core(q.reshape(B, S, -1), krep.reshape(B, S, -1), vrep.reshape(B, S, -1), attn_sinks[j])
            proj = (o.reshape(T, -1), a_w_out, j, a_b_out)
        else:
            ops = _gdn_in(xn.reshape(B, S, D), g_w_qkvz, g_w_b, g_w_a, gdn_conv_w, g_alog, g_dtb, j)
            o = _gdn_core(ops, g_nw, j)
            proj = (o.reshape(T, -1), g_w_out, j, None)
        last = layer == depth - 1
        h = _ffn(h, n2, w2gu, w2d, layer, proj=proj, final_w=final_norm[None, :] if last else None)
    return h.reshape(B, S, D)
```
